```python
import jax
import jax.numpy as jnp
from jax import lax
import numpy as np

D_MODEL = 4096
BATCH = 8
SEQ = 2048
DEPTH = 2

N_MIXERS = 2
N_MLSTM_LAYERS = (DEPTH + 1) // 2
N_POOL_LAYERS = DEPTH // 2
MLSTM_HEADS = 8
MLSTM_DV = D_MODEL // MLSTM_HEADS
MLSTM_DQK = MLSTM_DV // 2
MLSTM_QK_W = MLSTM_HEADS * MLSTM_DQK
MLSTM_V_W = MLSTM_HEADS * MLSTM_DV
MLSTM_SPLITS = (MLSTM_QK_W, 2 * MLSTM_QK_W, 2 * MLSTM_QK_W + MLSTM_V_W,
                2 * MLSTM_QK_W + 2 * MLSTM_V_W, 2 * MLSTM_QK_W + 2 * MLSTM_V_W + MLSTM_HEADS)
MLSTM_PROJ = 2 * MLSTM_QK_W + 2 * MLSTM_V_W + 2 * MLSTM_HEADS
MLSTM_CHUNK = 64
GATE_SOFTCAP = 15.0
POOL_WINDOWS = (2, 4, 8, 16)
POOL_GROUPS = len(POOL_WINDOWS)
POOL_GROUP_DIM = D_MODEL // POOL_GROUPS
D_FF = ((8 * D_MODEL // 3 + 63) // 64) * 64
FFN_CONV_WIDTH = 3
RMS_EPS = 1e-6

kernel_name = "hybrid_mlstm_pool_convffn"


def rmsnorm(x, g):
    xf = x.astype(jnp.float32)
    y = xf * lax.rsqrt(jnp.mean(xf * xf, axis=-1, keepdims=True) + RMS_EPS)
    return (y * g.astype(jnp.float32)).astype(x.dtype)


def softcap(a):
    return GATE_SOFTCAP * jnp.tanh(a / GATE_SOFTCAP)


def mlstm_chunkwise(q, k, v, i_pre, logf):
    B, H, S, _ = q.shape
    L = MLSTM_CHUNK
    NC = S // L

    def to_chunks(a):
        return jnp.moveaxis(a.reshape(B, H, NC, L, *a.shape[3:]), 2, 0)

    causal = jnp.tril(jnp.ones((L, L), dtype=bool))

    def step(carry, inp):
        C, n, m = carry
        qj, kj, vj, ij, fj = inp
        b = jnp.cumsum(fj, axis=-1)
        log_inter = b + m[..., None]
        log_intra = b[..., :, None] - b[..., None, :] + ij[..., None, :]
        log_intra = jnp.where(causal, log_intra, -jnp.inf)
        m_t = jnp.maximum(log_inter, jnp.max(log_intra, axis=-1))
        w_inter = jnp.exp(log_inter - m_t)
        w_intra = jnp.exp(log_intra - m_t[..., None])
        s = jnp.einsum("bhtd,bhsd->bhts", qj, kj) * w_intra
        num = (w_inter[..., None] * jnp.einsum("bhtd,bhde->bhte", qj, C)
               + jnp.einsum("bhts,bhse->bhte", s, vj))
        den = w_inter * jnp.einsum("bhtd,bhd->bht", qj, n) + jnp.sum(s, axis=-1)
        h = num / jnp.maximum(jnp.abs(den), jnp.exp(-m_t))[..., None]
        b_last = b[..., -1]
        log_prev = b_last + m
        log_in = b_last[..., None] - b + ij
        m_new = jnp.maximum(log_prev, jnp.max(log_in, axis=-1))
        a_prev = jnp.exp(log_prev - m_new)
        ka = jnp.exp(log_in - m_new[..., None])[..., None] * kj
        C_new = a_prev[..., None, None] * C + jnp.einsum("bhsd,bhse->bhde", ka, vj)
        n_new = a_prev[..., None] * n + jnp.sum(ka, axis=2)
        return (C_new, n_new, m_new), h

    init = (jnp.zeros((B, H, MLSTM_DQK, MLSTM_DV), jnp.float32),
            jnp.zeros((B, H, MLSTM_DQK), jnp.float32),
            jnp.zeros((B, H), jnp.float32))
    _, hs = lax.scan(step, init, (to_chunks(q), to_chunks(k), to_chunks(v),
                                  to_chunks(i_pre), to_chunks(logf)))
    return jnp.moveaxis(hs, 0, 2).reshape(B, H, S, MLSTM_DV)


def mlstm_mixer(xn, w_in, b_i, b_f, head_g, w_out):
    B, S, _ = xn.shape
    H = MLSTM_HEADS
    proj = jnp.einsum("bsd,dp->bsp", xn, w_in)
    q, k, v, o, ig, fg = jnp.split(proj, MLSTM_SPLITS, axis=-1)

    def heads(a, dh):
        return a.reshape(B, S, H, dh).transpose(0, 2, 1, 3).astype(jnp.float32)

    q = heads(q, MLSTM_DQK) * (MLSTM_DQK ** -0.5)
    k = heads(k, MLSTM_DQK)
    v = heads(v, MLSTM_DV)
    i_pre = softcap(ig.astype(jnp.float32) + b_i.astype(jnp.float32)).transpose(0, 2, 1)
    logf = jax.nn.log_sigmoid(softcap(fg.astype(jnp.float32) + b_f.astype(jnp.float32))).transpose(0, 2, 1)
    h = mlstm_chunkwise(q, k, v, i_pre, logf)
    h = h * lax.rsqrt(jnp.mean(h * h, axis=-1, keepdims=True) + RMS_EPS)
    h = h.transpose(0, 2, 1, 3).reshape(B, S, H * MLSTM_DV) * head_g.astype(jnp.float32)
    h = (h * jax.nn.sigmoid(o.astype(jnp.float32))).astype(xn.dtype)
    return jnp.einsum("bse,ed->bsd", h, w_out)


def pool_mixer(xn, w_group, scale):
    B, S, D = xn.shape
    xg = xn.astype(jnp.float32).reshape(B, S, POOL_GROUPS, POOL_GROUP_DIM)
    t = jnp.arange(1, S + 1, dtype=jnp.float32)
    outs = []
    for g, w in enumerate(POOL_WINDOWS):
        xs = xg[:, :, g]
        c = jnp.cumsum(xs, axis=1)
        c_shift = jnp.concatenate(
            [jnp.zeros((B, w, POOL_GROUP_DIM), jnp.float32), c[:, :S - w]], axis=1)
        cnt = jnp.minimum(t, float(w))[None, :, None]
        outs.append((c - c_shift) / cnt - xs)
    pooled = jnp.stack(outs, axis=2)
    y = jnp.einsum("bsgc,gce->bsge", pooled, w_group.astype(jnp.float32)).reshape(B, S, D)
    return (y * scale.astype(jnp.float32)).astype(xn.dtype)


def conv_ffn(xn, w_in, conv_w, conv_b, w_out):
    S = xn.shape[1]
    h = jnp.einsum("bsd,df->bsf", xn, w_in)
    hp = jnp.pad(h, ((0, 0), (FFN_CONV_WIDTH - 1, 0), (0, 0)))
    hc = conv_b
    for j in range(FFN_CONV_WIDTH):
        hc = hc + conv_w[j] * hp[:, j:j + S]
    g, u = jnp.split(hc, 2, axis=-1)
    return jnp.einsum("bsf,fd->bsd", jax.nn.silu(g) * u, w_out)


def setup_inputs(seed: int = 0) -> dict:
    key = jax.random.key(seed)
    ks = jax.random.split(key, 16)

    def nrm(k, shape, scale):
        return jax.random.normal(k, shape, jnp.float32) * scale

    NA, NB, H = N_MLSTM_LAYERS, N_POOL_LAYERS, MLSTM_HEADS
    return {
        "x": nrm(ks[0], (BATCH, SEQ, D_MODEL), 1.0),
        "mlstm_norm_g": 1.0 + nrm(ks[1], (NA, D_MODEL), 0.02),
        "mlstm_w_in": nrm(ks[2], (NA, D_MODEL, MLSTM_PROJ), D_MODEL ** -0.5),
        "mlstm_b_i": nrm(ks[3], (NA, H), 0.1),
        "mlstm_b_f": jnp.linspace(3.0, 6.0, H, dtype=jnp.float32)[None] + nrm(ks[4], (NA, H), 0.1),
        "mlstm_head_g": 1.0 + nrm(ks[5], (NA, H * MLSTM_DV), 0.02),
        "mlstm_w_out": nrm(ks[6], (NA, H * MLSTM_DV, D_MODEL), (H * MLSTM_DV) ** -0.5),
        "pool_norm_g": 1.0 + nrm(ks[7], (NB, D_MODEL), 0.02),
        "pool_w_group": nrm(ks[8], (NB, POOL_GROUPS, POOL_GROUP_DIM, POOL_GROUP_DIM), POOL_GROUP_DIM ** -0.5),
        "pool_scale": 1.0 + nrm(ks[9], (NB, D_MODEL), 0.1),
        "ffn_norm_g": 1.0 + nrm(ks[10], (DEPTH, D_MODEL), 0.02),
        "ffn_w_in": nrm(ks[11], (DEPTH, D_MODEL, 2 * D_FF), D_MODEL ** -0.5),
        "ffn_conv_w": nrm(ks[12], (DEPTH, FFN_CONV_WIDTH, 2 * D_FF), FFN_CONV_WIDTH ** -0.5),
        "ffn_conv_b": nrm(ks[13], (DEPTH, 2 * D_FF), 0.01),
        "ffn_w_out": nrm(ks[14], (DEPTH, D_FF, D_MODEL), D_FF ** -0.5),
        "final_norm_g": 1.0 + nrm(ks[15], (D_MODEL,), 0.02),
    }


def reference(x, mlstm_norm_g, mlstm_w_in, mlstm_b_i, mlstm_b_f, mlstm_head_g, mlstm_w_out,
              pool_norm_g, pool_w_group, pool_scale,
              ffn_norm_g, ffn_w_in, ffn_conv_w, ffn_conv_b, ffn_w_out, final_norm_g):
    for layer in range(DEPTH):
        j = layer // N_MIXERS
        if layer % N_MIXERS == 0:
            x = x + mlstm_mixer(rmsnorm(x, mlstm_norm_g[j]), mlstm_w_in[j], mlstm_b_i[j],
                                mlstm_b_f[j], mlstm_head_g[j], mlstm_w_out[j])
        else:
            x = x + pool_mixer(rmsnorm(x, pool_norm_g[j]), pool_w_group[j], pool_scale[j])
        x = x + conv_ffn(rmsnorm(x, ffn_norm_g[layer]), ffn_w_in[layer], ffn_conv_w[layer],
                         ffn_conv_b[layer], ffn_w_out[layer])
    return rmsnorm(x, final_norm_g)
```

```python
import functools

import jax
import jax.numpy as jnp
from jax import lax
from jax.experimental import pallas as pl
from jax.experimental.pallas import tpu as pltpu

RMS_EPS = 1e-6
GATE_SOFTCAP = 15.0
POOL_WINDOWS = (2, 4, 8, 16)
FFN_CONV_WIDTH = 3
MLSTM_CHUNK = 256
LANE = 128
SUBLANE = 8
POOL_HALO = 32
MIB = 1024 * 1024

F32 = jnp.float32
BF16 = jnp.bfloat16


def _params(semantics, vmem_mib):
    return pltpu.CompilerParams(dimension_semantics=semantics, vmem_limit_bytes=vmem_mib * MIB)


def _tile(dim, pref):
    t = min(dim, pref)
    while dim % t:
        t //= 2
    return t


def _rmsnorm_kernel(x_ref, g_ref, o_ref):
    x = x_ref[...]
    y = x * lax.rsqrt(jnp.mean(x * x, axis=-1, keepdims=True) + RMS_EPS)
    o_ref[...] = (y * g_ref[...]).astype(o_ref.dtype)


def rmsnorm(x2d, g, out_dtype):
    t, d = x2d.shape
    bm = _tile(t, 256)
    return pl.pallas_call(
        _rmsnorm_kernel,
        grid=(t // bm,),
        in_specs=[pl.BlockSpec((bm, d), lambda i: (i, 0)), pl.BlockSpec((1, d), lambda i: (0, 0))],
        out_specs=pl.BlockSpec((bm, d), lambda i: (i, 0)),
        out_shape=jax.ShapeDtypeStruct((t, d), out_dtype),
        compiler_params=_params(("parallel",), 32),
        name="rmsnorm",
    )(x2d, g.reshape(1, d))


def _mm_kernel(a_ref, w_ref, o_ref):
    o_ref[...] = jnp.dot(a_ref[...], w_ref[...], preferred_element_type=F32).astype(o_ref.dtype)


def matmul(a, w, out_dtype, bm_pref=1024, bn_pref=1024):
    m, k = a.shape
    n = w.shape[1]
    bm, bn = _tile(m, bm_pref), _tile(n, bn_pref)
    return pl.pallas_call(
        _mm_kernel,
        grid=(n // bn, m // bm),
        in_specs=[pl.BlockSpec((bm, k), lambda j, i: (i, 0)), pl.BlockSpec((k, bn), lambda j, i: (0, j))],
        out_specs=pl.BlockSpec((bm, bn), lambda j, i: (i, j)),
        out_shape=jax.ShapeDtypeStruct((m, n), out_dtype),
        compiler_params=_params(("parallel", "parallel"), 56),
        name="matmul",
    )(a, w)


def _mm_res_kernel(a_ref, w_ref, x_ref, o_ref, *, nk):
    kk = pl.program_id(2)
    p = jnp.dot(a_ref[...], w_ref[...], preferred_element_type=F32)

    @pl.when(kk == 0)
    def _():
        o_ref[...] = p

    @pl.when(kk > 0)
    def _():
        o_ref[...] += p

    @pl.when(kk == nk - 1)
    def _():
        o_ref[...] = x_ref[...] + o_ref[...]


def matmul_residual(a, w, x, bm_pref=1024, bn_pref=1024, bk_pref=1024):
    m, k = a.shape
    n = w.shape[1]
    bm, bn, bk = _tile(m, bm_pref), _tile(n, bn_pref), _tile(k, bk_pref)
    nk = k // bk
    return pl.pallas_call(
        functools.partial(_mm_res_kernel, nk=nk),
        grid=(m // bm, n // bn, nk),
        in_specs=[
            pl.BlockSpec((bm, bk), lambda i, j, kk: (i, kk)),
            pl.BlockSpec((bk, bn), lambda i, j, kk: (kk, j)),
            pl.BlockSpec((bm, bn), lambda i, j, kk: (i, j)),
        ],
        out_specs=pl.BlockSpec((bm, bn), lambda i, j, kk: (i, j)),
        out_shape=jax.ShapeDtypeStruct((m, n), F32),
        compiler_params=_params(("parallel", "parallel", "arbitrary"), 56),
        name="matmul_residual",
    )(a, w, x)


def _as_column(row, length):
    return jnp.transpose(jnp.broadcast_to(row, (LANE, length)))[:, 0:1]


def _cumsum_lanes(row, tri):
    x = jnp.broadcast_to(row, (SUBLANE, row.shape[1]))
    hi = x.astype(BF16).astype(F32)
    r1 = x - hi
    mid = r1.astype(BF16).astype(F32)
    lo = (r1 - mid).astype(BF16).astype(F32)
    parts = jnp.concatenate([hi, mid, lo], axis=0).astype(BF16)
    acc = jnp.dot(parts, tri, preferred_element_type=F32)
    s = acc[0:SUBLANE] + acc[SUBLANE:2 * SUBLANE] + acc[2 * SUBLANE:3 * SUBLANE]
    return s[0:1, :]


def _mlstm_kernel(bi_ref, bf_ref, q_ref, k_ref, v_ref, o_ref, gf_ref, hg_ref, tri_ref, out_ref,
                  c_ref, n_ref, m_ref, *, chunk, qk_scale):
    h = pl.program_id(1)

    @pl.when(pl.program_id(2) == 0)
    def _():
        c_ref[...] = jnp.zeros_like(c_ref)
        n_ref[...] = jnp.zeros_like(n_ref)
        m_ref[...] = jnp.zeros_like(m_ref)

    gf = gf_ref[...]
    i_row = GATE_SOFTCAP * jnp.tanh((gf[0:1, :] + bi_ref[h]) / GATE_SOFTCAP)
    logf = jax.nn.log_sigmoid(GATE_SOFTCAP * jnp.tanh((gf[1:2, :] + bf_ref[h]) / GATE_SOFTCAP))
    b_row = _cumsum_lanes(logf, tri_ref[...])
    d_row = i_row - b_row
    b_col = _as_column(b_row, chunk)
    d_col = _as_column(d_row, chunk)

    t_idx = lax.broadcasted_iota(jnp.int32, (chunk, chunk), 0)
    s_idx = lax.broadcasted_iota(jnp.int32, (chunk, chunk), 1)
    log_intra = jnp.where(t_idx >= s_idx, b_col + d_row, -jnp.inf)
    m_prev = m_ref[0:1, 0:1]
    log_inter = b_col + m_prev
    m_t = jnp.maximum(log_inter, jnp.max(log_intra, axis=1, keepdims=True))
    w_inter = jnp.exp(log_inter - m_t)
    w_intra = jnp.exp(log_intra - m_t)

    q = q_ref[...] * qk_scale
    k = k_ref[...]
    v = v_ref[...]
    s = lax.dot_general(q, k, (((1,), (1,)), ((), ())), preferred_element_type=F32) * w_intra
    c_prev = c_ref[...]
    n_prev = n_ref[...]
    num = (w_inter * jnp.dot(q, c_prev.astype(BF16), preferred_element_type=F32)
           + jnp.dot(s.astype(BF16), v, preferred_element_type=F32))
    qn = jnp.sum(q.astype(F32) * n_prev, axis=1, keepdims=True)
    den = w_inter * qn + jnp.sum(s, axis=1, keepdims=True)
    hval = num / jnp.maximum(jnp.abs(den), jnp.exp(-m_t))
    hval = hval * lax.rsqrt(jnp.mean(hval * hval, axis=-1, keepdims=True) + RMS_EPS)
    hval = hval * hg_ref[...]
    out_ref[...] = (hval * jax.nn.sigmoid(o_ref[...].astype(F32))).astype(out_ref.dtype)

    b_last = b_row[:, chunk - 1:chunk]
    log_prev = b_last + m_prev
    m_new = jnp.maximum(log_prev, jnp.max(b_last + d_row, axis=1, keepdims=True))
    a_prev = jnp.exp(log_prev - m_new)
    ka = jnp.exp(b_last + d_col - m_new) * k.astype(F32)
    c_ref[...] = a_prev * c_prev + lax.dot_general(
        ka.astype(BF16), v, (((0,), (0,)), ((), ())), preferred_element_type=F32)
    n_ref[...] = a_prev * n_prev + jnp.sum(ka, axis=0, keepdims=True)
    m_ref[...] = jnp.broadcast_to(m_new, m_ref.shape)


def mlstm_recurrence(proj, gates, b_i, b_f, head_g, batch, seq):
    heads = b_i.shape[0]
    dv = head_g.shape[0] // heads
    dqk = dv // 2
    chunk = _tile(seq, MLSTM_CHUNK)
    nc = seq // chunk
    proj3 = proj.reshape(batch, seq, proj.shape[1])
    gf = gates[:, :2 * heads].reshape(batch, nc, chunk, 2, heads).transpose(0, 4, 1, 3, 2)
    tri = (jnp.arange(chunk)[:, None] <= jnp.arange(chunk)[None, :]).astype(BF16)
    k_off = heads
    v_off = 2 * heads * dqk // dv
    o_off = v_off + heads
    smem = pl.BlockSpec(memory_space=pltpu.SMEM)
    out = pl.pallas_call(
        functools.partial(_mlstm_kernel, chunk=chunk, qk_scale=float(dqk) ** -0.5),
        grid=(batch, heads, nc),
        in_specs=[
            smem, smem,
            pl.BlockSpec((None, chunk, dqk), lambda b, h, c: (b, c, h)),
            pl.BlockSpec((None, chunk, dqk), lambda b, h, c: (b, c, k_off + h)),
            pl.BlockSpec((None, chunk, dv), lambda b, h, c: (b, c, v_off + h)),
            pl.BlockSpec((None, chunk, dv), lambda b, h, c: (b, c, o_off + h)),
            pl.BlockSpec((None, None, None, 2, chunk), lambda b, h, c: (b, h, c, 0, 0)),
            pl.BlockSpec((1, dv), lambda b, h, c: (0, h)),
            pl.BlockSpec((chunk, chunk), lambda b, h, c: (0, 0)),
        ],
        out_specs=pl.BlockSpec((None, chunk, dv), lambda b, h, c: (b, c, h)),
        out_shape=jax.ShapeDtypeStruct((batch, seq, heads * dv), BF16),
        scratch_shapes=[
            pltpu.VMEM((dqk, dv), F32),
            pltpu.VMEM((1, dqk), F32),
            pltpu.VMEM((SUBLANE, LANE), F32),
        ],
        compiler_params=_params(("parallel", "parallel", "arbitrary"), 32),
        name="mlstm_recurrence",
    )(b_i, b_f, proj3, proj3, proj3, proj3, gf, head_g.reshape(1, heads * dv), tri)
    return out.reshape(batch * seq, heads * dv)


def _ffn_in_kernel(a_ref, wg_ref, wu_ref, cwg_ref, cwu_ref, cbg_ref, cbu_ref, o_ref, hg_ref, hu_ref,
                   *, bm, tiles_per_seq):
    first = (pl.program_id(1) % tiles_per_seq) == 0

    @pl.when(first)
    def _():
        hg_ref[0:SUBLANE, :] = jnp.zeros((SUBLANE, hg_ref.shape[1]), F32)
        hu_ref[0:SUBLANE, :] = jnp.zeros((SUBLANE, hu_ref.shape[1]), F32)

    @pl.when(jnp.logical_not(first))
    def _():
        hg_ref[0:SUBLANE, :] = hg_ref[bm:bm + SUBLANE, :]
        hu_ref[0:SUBLANE, :] = hu_ref[bm:bm + SUBLANE, :]

    a = a_ref[...]

    def conv_half(w_ref, cw_ref, cb_ref, h_ref):
        hcur = jnp.dot(a, w_ref[...], preferred_element_type=F32)
        h_ref[SUBLANE:SUBLANE + bm, :] = hcur
        h1 = h_ref[SUBLANE - 1:SUBLANE - 1 + bm, :]
        h2 = h_ref[SUBLANE - 2:SUBLANE - 2 + bm, :]
        cw = cw_ref[...]
        return cb_ref[...] + cw[0:1, :] * h2 + cw[1:2, :] * h1 + cw[2:3, :] * hcur

    g = conv_half(wg_ref, cwg_ref, cbg_ref, hg_ref)
    u = conv_half(wu_ref, cwu_ref, cbu_ref, hu_ref)
    o_ref[...] = (jax.nn.silu(g) * u).astype(o_ref.dtype)


def ffn_in(a, w_in, conv_w, conv_b, seq, bm_pref=1024, bn_pref=512):
    t, d = a.shape
    fp = w_in.shape[1] // 2
    bm, bn = _tile(seq, bm_pref), _tile(fp, bn_pref)
    nj = fp // bn
    return pl.pallas_call(
        functools.partial(_ffn_in_kernel, bm=bm, tiles_per_seq=seq // bm),
        grid=(nj, t // bm),
        in_specs=[
            pl.BlockSpec((bm, d), lambda j, i: (i, 0)),
            pl.BlockSpec((d, bn), lambda j, i: (0, j)),
            pl.BlockSpec((d, bn), lambda j, i: (0, nj + j)),
            pl.BlockSpec((FFN_CONV_WIDTH, bn), lambda j, i: (0, j)),
            pl.BlockSpec((FFN_CONV_WIDTH, bn), lambda j, i: (0, nj + j)),
            pl.BlockSpec((1, bn), lambda j, i: (0, j)),
            pl.BlockSpec((1, bn), lambda j, i: (0, nj + j)),
        ],
        out_specs=pl.BlockSpec((bm, bn), lambda j, i: (i, j)),
        out_shape=jax.ShapeDtypeStruct((t, fp), BF16),
        scratch_shapes=[pltpu.VMEM((bm + 2 * SUBLANE, bn), F32), pltpu.VMEM((bm + 2 * SUBLANE, bn), F32)],
        compiler_params=_params(("parallel", "arbitrary"), 56),
        name="ffn_in",
    )(a, w_in, w_in, conv_w, conv_w, conv_b, conv_b)


def _pool_kernel(x_ref, g_ref, w_ref, sc_ref, o_ref, e_ref, p_ref, *, bm, tiles_per_seq, group_dim):
    step = pl.program_id(0)
    tile_in_seq = step % tiles_per_seq
    halo = POOL_HALO
    rows = halo + bm

    @pl.when(step == 0)
    def _():
        e_ref[0:halo, :] = jnp.zeros((halo, e_ref.shape[1]), F32)
        p_ref[0:SUBLANE, :] = jnp.zeros((SUBLANE, p_ref.shape[1]), F32)

    @pl.when(jnp.logical_and(step > 0, tile_in_seq == 0))
    def _():
        e_ref[0:halo, :] = jnp.zeros((halo, e_ref.shape[1]), F32)

    @pl.when(tile_in_seq > 0)
    def _():
        e_ref[0:halo, :] = e_ref[bm:bm + halo, :]

    x = x_ref[...]
    xn = x * lax.rsqrt(jnp.mean(x * x, axis=-1, keepdims=True) + RMS_EPS) * g_ref[...]
    e_ref[halo:rows, :] = xn

    pos = tile_in_seq * bm + lax.broadcasted_iota(jnp.int32, (bm, 1), 0)
    lo = SUBLANE
    for gi, window in enumerate(POOL_WINDOWS):
        cs = slice(gi * group_dim, (gi + 1) * group_dim)
        src, dst, span = e_ref, p_ref, 1
        while span < window:
            dst[lo:rows, cs] = src[lo:rows, cs] + src[lo - span:rows - span, cs]
            src, dst = dst, (e_ref if dst is p_ref else p_ref)
            span *= 2
        wsum = src[halo:rows, cs]
        cnt = jnp.minimum(pos + 1, window).astype(F32)
        pooled = wsum / cnt - xn[:, cs]
        y = jnp.dot(pooled.astype(BF16), w_ref[gi], preferred_element_type=F32)
        o_ref[:, cs] = x[:, cs] + y * sc_ref[:, cs]
        if window > 2:
            e_ref[halo:rows, cs] = xn[:, cs]


def pool_mixer(x2d, norm_g, w_group, scale, seq, bm_pref=256):
    t, d = x2d.shape
    groups, group_dim, _ = w_group.shape
    bm = _tile(seq, bm_pref)
    return pl.pallas_call(
        functools.partial(_pool_kernel, bm=bm, tiles_per_seq=seq // bm, group_dim=group_dim),
        grid=(t // bm,),
        in_specs=[
            pl.BlockSpec((bm, d), lambda i: (i, 0)),
            pl.BlockSpec((1, d), lambda i: (0, 0)),
            pl.BlockSpec((groups, group_dim, group_dim), lambda i: (0, 0, 0)),
            pl.BlockSpec((1, d), lambda i: (0, 0)),
        ],
        out_specs=pl.BlockSpec((bm, d), lambda i: (i, 0)),
        out_shape=jax.ShapeDtypeStruct((t, d), F32),
        scratch_shapes=[pltpu.VMEM((POOL_HALO + bm, d), F32), pltpu.VMEM((POOL_HALO + bm, d), F32)],
        compiler_params=_params(("arbitrary",), 56),
        name="pool_mixer",
    )(x2d, norm_g.reshape(1, d), w_group, scale.reshape(1, d))


def _pad_cols(w, width):
    return jnp.pad(w, [(0, 0)] * (w.ndim - 1) + [(0, width - w.shape[-1])])


def _conv_ffn(x2d, norm_g, w_in, conv_w, conv_b, w_out, seq):
    d, two_ff = w_in.shape
    d_ff = two_ff // 2
    fp = -(-d_ff // 1024) * 1024 if d_ff > 1024 else -(-d_ff // LANE) * LANE
    w_in_p = _pad_cols(w_in.reshape(d, 2, d_ff), fp).reshape(d, 2 * fp).astype(BF16)
    conv_w_p = _pad_cols(conv_w.reshape(FFN_CONV_WIDTH, 2, d_ff), fp).reshape(FFN_CONV_WIDTH, 2 * fp)
    conv_b_p = _pad_cols(conv_b.reshape(1, 2, d_ff), fp).reshape(1, 2 * fp)
    w_out_p = jnp.pad(w_out, ((0, fp - d_ff), (0, 0))).astype(BF16)
    xn = rmsnorm(x2d, norm_g, BF16)
    act = ffn_in(xn, w_in_p, conv_w_p, conv_b_p, seq)
    return matmul_residual(act, w_out_p, x2d)


def _mlstm_layer(x2d, norm_g, w_in, b_i, b_f, head_g, w_out, batch, seq):
    heads = b_i.shape[0]
    main = w_in.shape[1] - 2 * heads
    xn = rmsnorm(x2d, norm_g, BF16)
    proj = matmul(xn, w_in[:, :main].astype(BF16), BF16)
    gates = matmul(xn, _pad_cols(w_in[:, main:], LANE).astype(BF16), F32)
    hgated = mlstm_recurrence(proj, gates, b_i, b_f, head_g, batch, seq)
    return matmul_residual(hgated, w_out.astype(BF16), x2d)


def kernel(x, mlstm_norm_g, mlstm_w_in, mlstm_b_i, mlstm_b_f, mlstm_head_g, mlstm_w_out,
           pool_norm_g, pool_w_group, pool_scale,
           ffn_norm_g, ffn_w_in, ffn_conv_w, ffn_conv_b, ffn_w_out, final_norm_g):
    batch, seq, d = x.shape
    depth = ffn_w_in.shape[0]
    x2d = x.reshape(batch * seq, d)
    for layer in range(depth):
        j = layer // 2
        if layer % 2 == 0:
            x2d = _mlstm_layer(x2d, mlstm_norm_g[j], mlstm_w_in[j], mlstm_b_i[j], mlstm_b_f[j],
                               mlstm_head_g[j], mlstm_w_out[j], batch, seq)
        else:
            x2d = pool_mixer(x2d, pool_norm_g[j], pool_w_group[j].astype(BF16), pool_scale[j], seq)
        x2d = _conv_ffn(x2d, ffn_norm_g[layer], ffn_w_in[layer], ffn_conv_w[layer], ffn_conv_b[layer],
                        ffn_w_out[layer], seq)
    return rmsnorm(x2d, final_norm_g, x.dtype).reshape(batch, seq, d)
```

```python
import functools

import jax
import jax.numpy as jnp
from jax import lax
from jax.experimental import pallas as pl
from jax.experimental.pallas import tpu as pltpu

RMS_EPS = 1e-6
GATE_SOFTCAP = 15.0
POOL_WINDOWS = (2, 4, 8, 16)
FFN_CONV_WIDTH = 3
MLSTM_CHUNK = 256
LANE = 128
SUBLANE = 8
POOL_HALO = 32
MIB = 1024 * 1024

F32 = jnp.float32
BF16 = jnp.bfloat16


def _params(semantics, vmem_mib):
    return pltpu.CompilerParams(dimension_semantics=semantics, vmem_limit_bytes=vmem_mib * MIB)


def _tile(dim, pref):
    t = min(dim, pref)
    while dim % t:
        t //= 2
    return t


def _rmsnorm_kernel(x_ref, g_ref, o_ref):
    x = x_ref[...]
    y = x * lax.rsqrt(jnp.mean(x * x, axis=-1, keepdims=True) + RMS_EPS)
    o_ref[...] = (y * g_ref[...]).astype(o_ref.dtype)


def rmsnorm(x2d, g, out_dtype):
    t, d = x2d.shape
    bm = _tile(t, 256)
    return pl.pallas_call(
        _rmsnorm_kernel,
        grid=(t // bm,),
        in_specs=[pl.BlockSpec((bm, d), lambda i: (i, 0)), pl.BlockSpec((1, d), lambda i: (0, 0))],
        out_specs=pl.BlockSpec((bm, d), lambda i: (i, 0)),
        out_shape=jax.ShapeDtypeStruct((t, d), out_dtype),
        compiler_params=_params(("parallel",), 32),
        name="rmsnorm",
    )(x2d, g.reshape(1, d))


def _mm_kernel(a_ref, w_ref, o_ref):
    o_ref[...] = jnp.dot(a_ref[...], w_ref[...], preferred_element_type=F32).astype(o_ref.dtype)


def matmul(a, w, out_dtype, n=None, bm_pref=1024, bn_pref=1024):
    m, k = a.shape
    n = w.shape[1] if n is None else n
    bm, bn = _tile(m, bm_pref), _tile(n, bn_pref)
    return pl.pallas_call(
        _mm_kernel,
        grid=(n // bn, m // bm),
        in_specs=[pl.BlockSpec((bm, k), lambda j, i: (i, 0)), pl.BlockSpec((k, bn), lambda j, i: (0, j))],
        out_specs=pl.BlockSpec((bm, bn), lambda j, i: (i, j)),
        out_shape=jax.ShapeDtypeStruct((m, n), out_dtype),
        compiler_params=_params(("parallel", "parallel"), 56),
        name="matmul",
    )(a, w)


def _mm_res_kernel(a_ref, w_ref, x_ref, o_ref, *, nk):
    if nk == 1:
        o_ref[...] = x_ref[...] + jnp.dot(a_ref[...], w_ref[...], preferred_element_type=F32)
        return
    kk = pl.program_id(2)

    @pl.when(kk == 0)
    def _():
        o_ref[...] = jnp.zeros_like(o_ref)

    o_ref[...] += jnp.dot(a_ref[...], w_ref[...], preferred_element_type=F32)

    @pl.when(kk == nk - 1)
    def _():
        o_ref[...] = x_ref[...] + o_ref[...]


def matmul_residual(a, w, x, nk, bm_pref=1024, bn_pref=1024):
    m, k = a.shape
    n = w.shape[1]
    bm, bn, bk = _tile(m, bm_pref), _tile(n, bn_pref), k // nk
    assert bk * nk == k and (nk == 1 or bk % LANE == 0)
    return pl.pallas_call(
        functools.partial(_mm_res_kernel, nk=nk),
        grid=(m // bm, n // bn, nk),
        in_specs=[
            pl.BlockSpec((bm, bk), lambda i, j, kk: (i, kk)),
            pl.BlockSpec((bk, bn), lambda i, j, kk: (kk, j)),
            pl.BlockSpec((bm, bn), lambda i, j, kk: (i, j)),
        ],
        out_specs=pl.BlockSpec((bm, bn), lambda i, j, kk: (i, j)),
        out_shape=jax.ShapeDtypeStruct((m, n), F32),
        compiler_params=_params(("parallel", "parallel", "arbitrary"), 56),
        name="matmul_residual",
    )(a, w, x)


def _as_column(row, length):
    return jnp.transpose(jnp.broadcast_to(row, (LANE, length)))[:, 0:1]


def _cumsum_lanes(row, tri):
    x = jnp.broadcast_to(row, (SUBLANE, row.shape[1]))
    hi = x.astype(BF16).astype(F32)
    r1 = x - hi
    mid = r1.astype(BF16).astype(F32)
    lo = (r1 - mid).astype(BF16).astype(F32)
    parts = jnp.concatenate([hi, mid, lo], axis=0).astype(BF16)
    acc = jnp.dot(parts, tri, preferred_element_type=F32)
    s = acc[0:SUBLANE] + acc[SUBLANE:2 * SUBLANE] + acc[2 * SUBLANE:3 * SUBLANE]
    return s[0:1, :]


def _mlstm_kernel(bi_ref, bf_ref, q_ref, k_ref, v_ref, o_ref, gf_ref, hg_ref, tri_ref, out_ref,
                  c_ref, n_ref, m_ref, *, chunk, qk_scale):
    h = pl.program_id(1)

    @pl.when(pl.program_id(2) == 0)
    def _():
        c_ref[...] = jnp.zeros_like(c_ref)
        n_ref[...] = jnp.zeros_like(n_ref)
        m_ref[...] = jnp.zeros_like(m_ref)

    gf = gf_ref[...]
    i_row = GATE_SOFTCAP * jnp.tanh((gf[0:1, :] + bi_ref[h]) / GATE_SOFTCAP)
    logf = jax.nn.log_sigmoid(GATE_SOFTCAP * jnp.tanh((gf[1:2, :] + bf_ref[h]) / GATE_SOFTCAP))
    b_row = _cumsum_lanes(logf, tri_ref[...])
    d_row = i_row - b_row
    b_col = _as_column(b_row, chunk)
    d_col = _as_column(d_row, chunk)

    t_idx = lax.broadcasted_iota(jnp.int32, (chunk, chunk), 0)
    s_idx = lax.broadcasted_iota(jnp.int32, (chunk, chunk), 1)
    log_intra = jnp.where(t_idx >= s_idx, b_col + d_row, -jnp.inf)
    m_prev = m_ref[0:1, 0:1]
    log_inter = b_col + m_prev
    m_t = jnp.maximum(log_inter, jnp.max(log_intra, axis=1, keepdims=True))
    w_inter = jnp.exp(log_inter - m_t)
    w_intra = jnp.exp(log_intra - m_t)

    q = q_ref[...] * qk_scale
    k = k_ref[...]
    v = v_ref[...]
    s = lax.dot_general(q, k, (((1,), (1,)), ((), ())), preferred_element_type=F32) * w_intra
    c_prev = c_ref[...]
    n_prev = n_ref[...]
    num = (w_inter * jnp.dot(q, c_prev.astype(BF16), preferred_element_type=F32)
           + jnp.dot(s.astype(BF16), v, preferred_element_type=F32))
    qn = jnp.sum(q.astype(F32) * n_prev, axis=1, keepdims=True)
    den = w_inter * qn + jnp.sum(s, axis=1, keepdims=True)
    hval = num / jnp.maximum(jnp.abs(den), jnp.exp(-m_t))
    hval = hval * lax.rsqrt(jnp.mean(hval * hval, axis=-1, keepdims=True) + RMS_EPS)
    hval = hval * hg_ref[...]
    out_ref[...] = (hval * jax.nn.sigmoid(o_ref[...].astype(F32))).astype(out_ref.dtype)

    b_last = b_row[:, chunk - 1:chunk]
    log_prev = b_last + m_prev
    m_new = jnp.maximum(log_prev, jnp.max(b_last + d_row, axis=1, keepdims=True))
    a_prev = jnp.exp(log_prev - m_new)
    ka = jnp.exp(b_last + d_col - m_new) * k.astype(F32)
    c_ref[...] = a_prev * c_prev + lax.dot_general(
        ka.astype(BF16), v, (((0,), (0,)), ((), ())), preferred_element_type=F32)
    n_ref[...] = a_prev * n_prev + jnp.sum(ka, axis=0, keepdims=True)
    m_ref[...] = jnp.broadcast_to(m_new, m_ref.shape)


def mlstm_recurrence(proj, gates, b_i, b_f, head_g, batch, seq):
    heads = b_i.shape[0]
    dv = head_g.shape[0] // heads
    dqk = dv // 2
    chunk = _tile(seq, MLSTM_CHUNK)
    nc = seq // chunk
    proj3 = proj.reshape(batch, seq, proj.shape[1])
    gf = gates[:, :2 * heads].reshape(batch, nc, chunk, 2, heads).transpose(0, 4, 1, 3, 2)
    tri = (jnp.arange(chunk)[:, None] <= jnp.arange(chunk)[None, :]).astype(BF16)
    k_off = heads
    v_off = 2 * heads * dqk // dv
    o_off = v_off + heads
    smem = pl.BlockSpec(memory_space=pltpu.SMEM)
    out = pl.pallas_call(
        functools.partial(_mlstm_kernel, chunk=chunk, qk_scale=float(dqk) ** -0.5),
        grid=(batch, heads, nc),
        in_specs=[
            smem, smem,
            pl.BlockSpec((None, chunk, dqk), lambda b, h, c: (b, c, h)),
            pl.BlockSpec((None, chunk, dqk), lambda b, h, c: (b, c, k_off + h)),
            pl.BlockSpec((None, chunk, dv), lambda b, h, c: (b, c, v_off + h)),
            pl.BlockSpec((None, chunk, dv), lambda b, h, c: (b, c, o_off + h)),
            pl.BlockSpec((None, None, None, 2, chunk), lambda b, h, c: (b, h, c, 0, 0)),
            pl.BlockSpec((1, dv), lambda b, h, c: (0, h)),
            pl.BlockSpec((chunk, chunk), lambda b, h, c: (0, 0)),
        ],
        out_specs=pl.BlockSpec((None, chunk, dv), lambda b, h, c: (b, c, h)),
        out_shape=jax.ShapeDtypeStruct((batch, seq, heads * dv), BF16),
        scratch_shapes=[
            pltpu.VMEM((dqk, dv), F32),
            pltpu.VMEM((1, dqk), F32),
            pltpu.VMEM((SUBLANE, LANE), F32),
        ],
        compiler_params=_params(("parallel", "parallel", "arbitrary"), 32),
        name="mlstm_recurrence",
    )(b_i, b_f, proj3, proj3, proj3, proj3, gf, head_g.reshape(1, heads * dv), tri)
    return out.reshape(batch * seq, heads * dv)


def _ffn_in_kernel(a_ref, wg_ref, wu_ref, cwg_ref, cwu_ref, cbg_ref, cbu_ref, o_ref, hg_ref, hu_ref,
                   *, bm, tiles_per_seq):
    first = (pl.program_id(1) % tiles_per_seq) == 0

    @pl.when(first)
    def _():
        hg_ref[0:SUBLANE, :] = jnp.zeros((SUBLANE, hg_ref.shape[1]), F32)
        hu_ref[0:SUBLANE, :] = jnp.zeros((SUBLANE, hu_ref.shape[1]), F32)

    @pl.when(jnp.logical_not(first))
    def _():
        hg_ref[0:SUBLANE, :] = hg_ref[bm:bm + SUBLANE, :]
        hu_ref[0:SUBLANE, :] = hu_ref[bm:bm + SUBLANE, :]

    a = a_ref[...]

    def conv_half(w_ref, cw_ref, cb_ref, h_ref):
        hcur = jnp.dot(a, w_ref[...], preferred_element_type=F32)
        h_ref[SUBLANE:SUBLANE + bm, :] = hcur
        h1 = h_ref[SUBLANE - 1:SUBLANE - 1 + bm, :]
        h2 = h_ref[SUBLANE - 2:SUBLANE - 2 + bm, :]
        cw = cw_ref[...]
        return cb_ref[...] + cw[0:1, :] * h2 + cw[1:2, :] * h1 + cw[2:3, :] * hcur

    g = conv_half(wg_ref, cwg_ref, cbg_ref, hg_ref)
    u = conv_half(wu_ref, cwu_ref, cbu_ref, hu_ref)
    o_ref[...] = (jax.nn.silu(g) * u).astype(o_ref.dtype)


def ffn_in(a, w_in, conv_w, conv_b, seq, bm_pref=1024, bn_pref=512):
    t, d = a.shape
    fp = w_in.shape[1] // 2
    bm, bn = _tile(seq, bm_pref), _tile(fp, bn_pref)
    nj = fp // bn
    return pl.pallas_call(
        functools.partial(_ffn_in_kernel, bm=bm, tiles_per_seq=seq // bm),
        grid=(nj, t // bm),
        in_specs=[
            pl.BlockSpec((bm, d), lambda j, i: (i, 0)),
            pl.BlockSpec((d, bn), lambda j, i: (0, j)),
            pl.BlockSpec((d, bn), lambda j, i: (0, nj + j)),
            pl.BlockSpec((FFN_CONV_WIDTH, bn), lambda j, i: (0, j)),
            pl.BlockSpec((FFN_CONV_WIDTH, bn), lambda j, i: (0, nj + j)),
            pl.BlockSpec((1, bn), lambda j, i: (0, j)),
            pl.BlockSpec((1, bn), lambda j, i: (0, nj + j)),
        ],
        out_specs=pl.BlockSpec((bm, bn), lambda j, i: (i, j)),
        out_shape=jax.ShapeDtypeStruct((t, fp), BF16),
        scratch_shapes=[pltpu.VMEM((bm + 2 * SUBLANE, bn), F32), pltpu.VMEM((bm + 2 * SUBLANE, bn), F32)],
        compiler_params=_params(("parallel", "arbitrary"), 56),
        name="ffn_in",
    )(a, w_in, w_in, conv_w, conv_w, conv_b, conv_b)


def _pool_kernel(x_ref, g_ref, w_ref, sc_ref, o_ref, e_ref, p_ref, *, bm, tiles_per_seq, group_dim):
    step = pl.program_id(0)
    tile_in_seq = step % tiles_per_seq
    halo = POOL_HALO
    rows = halo + bm

    @pl.when(step == 0)
    def _():
        e_ref[0:halo, :] = jnp.zeros((halo, e_ref.shape[1]), F32)
        p_ref[0:SUBLANE, :] = jnp.zeros((SUBLANE, p_ref.shape[1]), F32)

    @pl.when(jnp.logical_and(step > 0, tile_in_seq == 0))
    def _():
        e_ref[0:halo, :] = jnp.zeros((halo, e_ref.shape[1]), F32)

    @pl.when(tile_in_seq > 0)
    def _():
        e_ref[0:halo, :] = e_ref[bm:bm + halo, :]

    x = x_ref[...]
    xn = x * lax.rsqrt(jnp.mean(x * x, axis=-1, keepdims=True) + RMS_EPS) * g_ref[...]
    e_ref[halo:rows, :] = xn

    pos = tile_in_seq * bm + lax.broadcasted_iota(jnp.int32, (bm, 1), 0)
    lo = SUBLANE
    for gi, window in enumerate(POOL_WINDOWS):
        cs = slice(gi * group_dim, (gi + 1) * group_dim)
        src, dst, span = e_ref, p_ref, 1
        while span < window:
            dst[lo:rows, cs] = src[lo:rows, cs] + src[lo - span:rows - span, cs]
            src, dst = dst, (e_ref if dst is p_ref else p_ref)
            span *= 2
        wsum = src[halo:rows, cs]
        cnt = jnp.minimum(pos + 1, window).astype(F32)
        pooled = wsum / cnt - xn[:, cs]
        y = jnp.dot(pooled.astype(BF16), w_ref[gi], preferred_element_type=F32)
        o_ref[:, cs] = x[:, cs] + y * sc_ref[:, cs]
        if window > 2:
            e_ref[halo:rows, cs] = xn[:, cs]


def pool_mixer(x2d, norm_g, w_group, scale, seq, bm_pref=256):
    t, d = x2d.shape
    groups, group_dim, _ = w_group.shape
    bm = _tile(seq, bm_pref)
    return pl.pallas_call(
        functools.partial(_pool_kernel, bm=bm, tiles_per_seq=seq // bm, group_dim=group_dim),
        grid=(t // bm,),
        in_specs=[
            pl.BlockSpec((bm, d), lambda i: (i, 0)),
            pl.BlockSpec((1, d), lambda i: (0, 0)),
            pl.BlockSpec((groups, group_dim, group_dim), lambda i: (0, 0, 0)),
            pl.BlockSpec((1, d), lambda i: (0, 0)),
        ],
        out_specs=pl.BlockSpec((bm, d), lambda i: (i, 0)),
        out_shape=jax.ShapeDtypeStruct((t, d), F32),
        scratch_shapes=[pltpu.VMEM((POOL_HALO + bm, d), F32), pltpu.VMEM((POOL_HALO + bm, d), F32)],
        compiler_params=_params(("arbitrary",), 56),
        name="pool_mixer",
    )(x2d, norm_g.reshape(1, d), w_group, scale.reshape(1, d))


def _prep_w_in_kernel(w_ref, o_ref, *, d_ff, fp):
    rows = w_ref.shape[0]
    full = (d_ff // LANE) * LANE
    tail = d_ff - full
    x = w_ref[...]
    o_ref[:, 0:full] = x[:, 0:full].astype(BF16)
    if tail == 0:
        up = x[:, d_ff:2 * d_ff]
    else:
        keep = lax.broadcasted_iota(jnp.int32, (rows, LANE), 1) < tail
        o_ref[:, full:full + LANE] = jnp.where(keep, x[:, full:full + LANE], 0.0).astype(BF16)
        src = x[:, full:2 * d_ff]
        up = pltpu.roll(src, src.shape[1] - tail, axis=1)
        o_ref[:, fp + full:fp + full + LANE] = jnp.where(keep, up[:, full:full + LANE], 0.0).astype(BF16)
    o_ref[:, fp:fp + full] = up[:, 0:full].astype(BF16)
    used = full + (LANE if tail else 0)
    if used < fp:
        zeros = jnp.zeros((rows, fp - used), BF16)
        o_ref[:, used:fp] = zeros
        o_ref[:, fp + used:2 * fp] = zeros


def prep_w_in(w_in_all, layer, fp):
    _, d, two_ff = w_in_all.shape
    d_ff = two_ff // 2
    rows = _tile(d, 128)
    return pl.pallas_call(
        functools.partial(_prep_w_in_kernel, d_ff=d_ff, fp=fp),
        grid=(d // rows,),
        in_specs=[pl.BlockSpec((None, rows, two_ff), lambda i: (layer, i, 0))],
        out_specs=pl.BlockSpec((rows, 2 * fp), lambda i: (i, 0)),
        out_shape=jax.ShapeDtypeStruct((d, 2 * fp), BF16),
        compiler_params=_params(("parallel",), 48),
        name="prep_w_in",
    )(w_in_all)


def _prep_w_out_kernel(w_ref, o_ref, *, n_src):
    @pl.when(pl.program_id(0) < n_src)
    def _():
        o_ref[...] = w_ref[...].astype(BF16)

    @pl.when(pl.program_id(0) >= n_src)
    def _():
        o_ref[...] = jnp.zeros_like(o_ref)


def prep_w_out(w_out_all, layer, fp):
    _, d_ff, d = w_out_all.shape
    rows = 64
    assert d_ff % rows == 0 and fp % rows == 0
    n_src = d_ff // rows
    return pl.pallas_call(
        functools.partial(_prep_w_out_kernel, n_src=n_src),
        grid=(fp // rows,),
        in_specs=[pl.BlockSpec((None, rows, d), lambda i: (layer, jnp.minimum(i, n_src - 1), 0))],
        out_specs=pl.BlockSpec((rows, d), lambda i: (i, 0)),
        out_shape=jax.ShapeDtypeStruct((fp, d), BF16),
        compiler_params=_params(("parallel",), 32),
        name="prep_w_out",
    )(w_out_all)


def _pad_cols(w, width):
    return jnp.pad(w, [(0, 0)] * (w.ndim - 1) + [(0, width - w.shape[-1])])


FFN_OUT_K_STEPS = 4


def _padded_hidden(d_ff):
    unit = 1024 if d_ff > 1024 else LANE * FFN_OUT_K_STEPS
    return -(-d_ff // unit) * unit


def _conv_ffn(x2d, norm_g, w_in_all, conv_w, conv_b, w_out_all, layer, seq):
    d_ff = w_out_all.shape[1]
    fp = _padded_hidden(d_ff)
    w_in_p = prep_w_in(w_in_all, layer, fp)
    w_out_p = prep_w_out(w_out_all, layer, fp)
    conv_w_p = _pad_cols(conv_w.reshape(FFN_CONV_WIDTH, 2, d_ff), fp).reshape(FFN_CONV_WIDTH, 2 * fp)
    conv_b_p = _pad_cols(conv_b.reshape(1, 2, d_ff), fp).reshape(1, 2 * fp)
    xn = rmsnorm(x2d, norm_g, BF16)
    act = ffn_in(xn, w_in_p, conv_w_p, conv_b_p, seq)
    return matmul_residual(act, w_out_p, x2d, FFN_OUT_K_STEPS)


def _mlstm_layer(x2d, norm_g, w_in, b_i, b_f, head_g, w_out, batch, seq):
    heads = b_i.shape[0]
    main = w_in.shape[1] - 2 * heads
    xn = rmsnorm(x2d, norm_g, BF16)
    proj = matmul(xn, w_in.astype(BF16), BF16, n=main)
    gates = matmul(xn, _pad_cols(w_in[:, main:], LANE).astype(BF16), F32)
    hgated = mlstm_recurrence(proj, gates, b_i, b_f, head_g, batch, seq)
    return matmul_residual(hgated, w_out.astype(BF16), x2d, 1)


def kernel(x, mlstm_norm_g, mlstm_w_in, mlstm_b_i, mlstm_b_f, mlstm_head_g, mlstm_w_out,
           pool_norm_g, pool_w_group, pool_scale,
           ffn_norm_g, ffn_w_in, ffn_conv_w, ffn_conv_b, ffn_w_out, final_norm_g):
    batch, seq, d = x.shape
    depth = ffn_w_in.shape[0]
    x2d = x.reshape(batch * seq, d)
    for layer in range(depth):
        j = layer // 2
        if layer % 2 == 0:
            x2d = _mlstm_layer(x2d, mlstm_norm_g[j], mlstm_w_in[j], mlstm_b_i[j], mlstm_b_f[j],
                               mlstm_head_g[j], mlstm_w_out[j], batch, seq)
        else:
            x2d = pool_mixer(x2d, pool_norm_g[j], pool_w_group[j].astype(BF16), pool_scale[j], seq)
        x2d = _conv_ffn(x2d, ffn_norm_g[layer], ffn_w_in, ffn_conv_w[layer], ffn_conv_b[layer],
                        ffn_w_out, layer, seq)
    return rmsnorm(x2d, final_norm_g, x.dtype).reshape(batch, seq, d)
```

```python
import functools

import jax
import jax.numpy as jnp
from jax import lax
from jax.experimental import pallas as pl
from jax.experimental.pallas import tpu as pltpu

RMS_EPS = 1e-6
GATE_SOFTCAP = 15.0
POOL_WINDOWS = (2, 4, 8, 16)
FFN_CONV_WIDTH = 3
MLSTM_CHUNK = 256
MLSTM_HEADS_PER_STEP = 2
LANE = 128
SUBLANE = 8
POOL_HALO = 32
MIB = 1024 * 1024

F32 = jnp.float32
BF16 = jnp.bfloat16


def _params(semantics, vmem_mib):
    return pltpu.CompilerParams(dimension_semantics=semantics, vmem_limit_bytes=vmem_mib * MIB)


def _tile(dim, pref):
    t = min(dim, pref)
    while dim % t:
        t //= 2
    return t


def _rmsnorm_kernel(x_ref, g_ref, o_ref):
    x = x_ref[...]
    y = x * lax.rsqrt(jnp.mean(x * x, axis=-1, keepdims=True) + RMS_EPS)
    o_ref[...] = (y * g_ref[...]).astype(o_ref.dtype)


def rmsnorm(x2d, g, out_dtype):
    t, d = x2d.shape
    bm = _tile(t, 256)
    return pl.pallas_call(
        _rmsnorm_kernel,
        grid=(t // bm,),
        in_specs=[pl.BlockSpec((bm, d), lambda i: (i, 0)), pl.BlockSpec((1, d), lambda i: (0, 0))],
        out_specs=pl.BlockSpec((bm, d), lambda i: (i, 0)),
        out_shape=jax.ShapeDtypeStruct((t, d), out_dtype),
        compiler_params=_params(("parallel",), 32),
        name="rmsnorm",
    )(x2d, g.reshape(1, d))


def _mm_kernel(a_ref, w_ref, o_ref):
    o_ref[...] = jnp.dot(a_ref[...], w_ref[...], preferred_element_type=F32).astype(o_ref.dtype)


def matmul(a, w, out_dtype, n=None, bm_pref=1024, bn_pref=1024):
    m, k = a.shape
    n = w.shape[1] if n is None else n
    bm, bn = _tile(m, bm_pref), _tile(n, bn_pref)
    return pl.pallas_call(
        _mm_kernel,
        grid=(n // bn, m // bm),
        in_specs=[pl.BlockSpec((bm, k), lambda j, i: (i, 0)), pl.BlockSpec((k, bn), lambda j, i: (0, j))],
        out_specs=pl.BlockSpec((bm, bn), lambda j, i: (i, j)),
        out_shape=jax.ShapeDtypeStruct((m, n), out_dtype),
        compiler_params=_params(("parallel", "parallel"), 56),
        name="matmul",
    )(a, w)


def _mm_res_kernel(a_ref, w_ref, x_ref, o_ref, *, nk):
    def partial_product():
        return jnp.dot(a_ref[...], w_ref[...], preferred_element_type=F32)

    if nk == 1:
        o_ref[...] = x_ref[...] + partial_product()
        return
    kk = pl.program_id(2)

    @pl.when(kk == 0)
    def _():
        o_ref[...] = partial_product()

    @pl.when(jnp.logical_and(kk > 0, kk < nk - 1))
    def _():
        o_ref[...] += partial_product()

    @pl.when(kk == nk - 1)
    def _():
        o_ref[...] = x_ref[...] + (o_ref[...] + partial_product())


def matmul_residual(a, w, x, nk, bm_pref=1024, bn_pref=1024):
    m, k = a.shape
    n = w.shape[1]
    bm, bn, bk = _tile(m, bm_pref), _tile(n, bn_pref), k // nk
    assert bk * nk == k and (nk == 1 or bk % LANE == 0)
    return pl.pallas_call(
        functools.partial(_mm_res_kernel, nk=nk),
        grid=(m // bm, n // bn, nk),
        in_specs=[
            pl.BlockSpec((bm, bk), lambda i, j, kk: (i, kk)),
            pl.BlockSpec((bk, bn), lambda i, j, kk: (kk, j)),
            pl.BlockSpec((bm, bn), lambda i, j, kk: (i, j)),
        ],
        out_specs=pl.BlockSpec((bm, bn), lambda i, j, kk: (i, j)),
        out_shape=jax.ShapeDtypeStruct((m, n), F32),
        compiler_params=_params(("parallel", "parallel", "arbitrary"), 56),
        name="matmul_residual",
    )(a, w, x)


def _as_column(row, length):
    return jnp.transpose(jnp.broadcast_to(row, (LANE, length)))[:, 0:1]


def _cumsum_lanes(row, tri):
    x = jnp.broadcast_to(row, (SUBLANE, row.shape[1]))
    hi = x.astype(BF16).astype(F32)
    r1 = x - hi
    mid = r1.astype(BF16).astype(F32)
    lo = (r1 - mid).astype(BF16).astype(F32)
    parts = jnp.concatenate([hi, mid, lo], axis=0).astype(BF16)
    acc = jnp.dot(parts, tri, preferred_element_type=F32)
    s = acc[0:SUBLANE] + acc[SUBLANE:2 * SUBLANE] + acc[2 * SUBLANE:3 * SUBLANE]
    return s[0:1, :]


def _mlstm_kernel(bi_ref, bf_ref, q_ref, k_ref, v_ref, o_ref, gf_ref, hg_ref, tri_ref, out_ref,
                  c_ref, n_ref, m_ref, *, chunk, dqk, dv, heads_per_step, qk_scale):
    @pl.when(pl.program_id(2) == 0)
    def _():
        c_ref[...] = jnp.zeros_like(c_ref)
        n_ref[...] = jnp.zeros_like(n_ref)
        m_ref[...] = jnp.zeros_like(m_ref)

    tri = tri_ref[...]
    t_idx = lax.broadcasted_iota(jnp.int32, (chunk, chunk), 0)
    s_idx = lax.broadcasted_iota(jnp.int32, (chunk, chunk), 1)
    causal = t_idx >= s_idx

    for hh in range(heads_per_step):
        h = pl.program_id(1) * heads_per_step + hh
        qs = slice(hh * dqk, (hh + 1) * dqk)
        vs = slice(hh * dv, (hh + 1) * dv)
        gf = gf_ref[hh]
        i_row = GATE_SOFTCAP * jnp.tanh((gf[0:1, :] + bi_ref[h]) / GATE_SOFTCAP)
        logf = jax.nn.log_sigmoid(GATE_SOFTCAP * jnp.tanh((gf[1:2, :] + bf_ref[h]) / GATE_SOFTCAP))
        b_row = _cumsum_lanes(logf, tri)
        d_row = i_row - b_row
        b_col = _as_column(b_row, chunk)
        d_col = _as_column(d_row, chunk)

        log_intra = jnp.where(causal, b_col + d_row, -jnp.inf)
        m_prev = m_ref[hh, 0:1, 0:1]
        log_inter = b_col + m_prev
        m_t = jnp.maximum(log_inter, jnp.max(log_intra, axis=1, keepdims=True))
        w_inter = jnp.exp(log_inter - m_t)
        w_intra = jnp.exp(log_intra - m_t)

        q = q_ref[:, qs] * qk_scale
        k = k_ref[:, qs]
        v = v_ref[:, vs]
        s = lax.dot_general(q, k, (((1,), (1,)), ((), ())), preferred_element_type=F32) * w_intra
        c_prev = c_ref[hh]
        n_prev = n_ref[hh]
        num = (w_inter * jnp.dot(q, c_prev.astype(BF16), preferred_element_type=F32)
               + jnp.dot(s.astype(BF16), v, preferred_element_type=F32))
        qn = jnp.sum(q.astype(F32) * n_prev, axis=1, keepdims=True)
        den = w_inter * qn + jnp.sum(s, axis=1, keepdims=True)
        hval = num / jnp.maximum(jnp.abs(den), jnp.exp(-m_t))
        hval = hval * lax.rsqrt(jnp.mean(hval * hval, axis=-1, keepdims=True) + RMS_EPS)
        hval = hval * hg_ref[:, vs]
        out_ref[:, vs] = (hval * jax.nn.sigmoid(o_ref[:, vs].astype(F32))).astype(out_ref.dtype)

        b_last = b_row[:, chunk - 1:chunk]
        log_prev = b_last + m_prev
        m_new = jnp.maximum(log_prev, jnp.max(b_last + d_row, axis=1, keepdims=True))
        a_prev = jnp.exp(log_prev - m_new)
        ka = jnp.exp(b_last + d_col - m_new) * k.astype(F32)
        c_ref[hh] = a_prev * c_prev + lax.dot_general(
            ka.astype(BF16), v, (((0,), (0,)), ((), ())), preferred_element_type=F32)
        n_ref[hh] = a_prev * n_prev + jnp.sum(ka, axis=0, keepdims=True)
        m_ref[hh] = jnp.broadcast_to(m_new, m_ref.shape[1:])


def mlstm_recurrence(proj, gates, b_i, b_f, head_g, batch, seq):
    heads = b_i.shape[0]
    dv = head_g.shape[0] // heads
    dqk = dv // 2
    chunk = _tile(seq, MLSTM_CHUNK)
    nc = seq // chunk
    hps = _tile(heads, MLSTM_HEADS_PER_STEP)
    proj3 = proj.reshape(batch, seq, proj.shape[1])
    gf = gates[:, :2 * heads].reshape(batch, nc, chunk, 2, heads).transpose(0, 4, 1, 3, 2)
    tri = (jnp.arange(chunk)[:, None] <= jnp.arange(chunk)[None, :]).astype(BF16)
    k_off = heads // hps
    v_off = 2 * heads * dqk // (dv * hps)
    o_off = v_off + heads // hps
    smem = pl.BlockSpec(memory_space=pltpu.SMEM)
    out = pl.pallas_call(
        functools.partial(_mlstm_kernel, chunk=chunk, dqk=dqk, dv=dv, heads_per_step=hps,
                          qk_scale=float(dqk) ** -0.5),
        grid=(batch, heads // hps, nc),
        in_specs=[
            smem, smem,
            pl.BlockSpec((None, chunk, hps * dqk), lambda b, h, c: (b, c, h)),
            pl.BlockSpec((None, chunk, hps * dqk), lambda b, h, c: (b, c, k_off + h)),
            pl.BlockSpec((None, chunk, hps * dv), lambda b, h, c: (b, c, v_off + h)),
            pl.BlockSpec((None, chunk, hps * dv), lambda b, h, c: (b, c, o_off + h)),
            pl.BlockSpec((None, hps, None, 2, chunk), lambda b, h, c: (b, h, c, 0, 0)),
            pl.BlockSpec((1, hps * dv), lambda b, h, c: (0, h)),
            pl.BlockSpec((chunk, chunk), lambda b, h, c: (0, 0)),
        ],
        out_specs=pl.BlockSpec((None, chunk, hps * dv), lambda b, h, c: (b, c, h)),
        out_shape=jax.ShapeDtypeStruct((batch, seq, heads * dv), BF16),
        scratch_shapes=[
            pltpu.VMEM((hps, dqk, dv), F32),
            pltpu.VMEM((hps, 1, dqk), F32),
            pltpu.VMEM((hps, SUBLANE, LANE), F32),
        ],
        compiler_params=_params(("parallel", "parallel", "arbitrary"), 48),
        name="mlstm_recurrence",
    )(b_i, b_f, proj3, proj3, proj3, proj3, gf, head_g.reshape(1, heads * dv), tri)
    return out.reshape(batch * seq, heads * dv)


def _ffn_in_kernel(a_ref, wg_ref, wu_ref, cwg_ref, cwu_ref, cbg_ref, cbu_ref, o_ref, hg_ref, hu_ref,
                   *, bm, tiles_per_seq):
    first = (pl.program_id(1) % tiles_per_seq) == 0

    @pl.when(first)
    def _():
        hg_ref[0:SUBLANE, :] = jnp.zeros((SUBLANE, hg_ref.shape[1]), F32)
        hu_ref[0:SUBLANE, :] = jnp.zeros((SUBLANE, hu_ref.shape[1]), F32)

    @pl.when(jnp.logical_not(first))
    def _():
        hg_ref[0:SUBLANE, :] = hg_ref[bm:bm + SUBLANE, :]
        hu_ref[0:SUBLANE, :] = hu_ref[bm:bm + SUBLANE, :]

    a = a_ref[...]

    def conv_half(w_ref, cw_ref, cb_ref, h_ref):
        hcur = jnp.dot(a, w_ref[...], preferred_element_type=F32)
        h_ref[SUBLANE:SUBLANE + bm, :] = hcur
        hall = h_ref[0:SUBLANE + bm, :]
        h1 = pltpu.roll(hall, 1, axis=0)[SUBLANE:, :]
        h2 = pltpu.roll(hall, 2, axis=0)[SUBLANE:, :]
        cw = cw_ref[...]
        return cb_ref[...] + cw[0:1, :] * h2 + cw[1:2, :] * h1 + cw[2:3, :] * hcur

    g = conv_half(wg_ref, cwg_ref, cbg_ref, hg_ref)
    u = conv_half(wu_ref, cwu_ref, cbu_ref, hu_ref)
    o_ref[...] = (jax.nn.silu(g) * u).astype(o_ref.dtype)


def ffn_in(a, w_in, conv_w, conv_b, seq, bm_pref=1024, bn_pref=512):
    t, d = a.shape
    fp = w_in.shape[1] // 2
    bm, bn = _tile(seq, bm_pref), _tile(fp, bn_pref)
    nj = fp // bn
    return pl.pallas_call(
        functools.partial(_ffn_in_kernel, bm=bm, tiles_per_seq=seq // bm),
        grid=(nj, t // bm),
        in_specs=[
            pl.BlockSpec((bm, d), lambda j, i: (i, 0)),
            pl.BlockSpec((d, bn), lambda j, i: (0, j)),
            pl.BlockSpec((d, bn), lambda j, i: (0, nj + j)),
            pl.BlockSpec((FFN_CONV_WIDTH, bn), lambda j, i: (0, j)),
            pl.BlockSpec((FFN_CONV_WIDTH, bn), lambda j, i: (0, nj + j)),
            pl.BlockSpec((1, bn), lambda j, i: (0, j)),
            pl.BlockSpec((1, bn), lambda j, i: (0, nj + j)),
        ],
        out_specs=pl.BlockSpec((bm, bn), lambda j, i: (i, j)),
        out_shape=jax.ShapeDtypeStruct((t, fp), BF16),
        scratch_shapes=[pltpu.VMEM((bm + 2 * SUBLANE, bn), F32), pltpu.VMEM((bm + 2 * SUBLANE, bn), F32)],
        compiler_params=_params(("parallel", "arbitrary"), 56),
        name="ffn_in",
    )(a, w_in, w_in, conv_w, conv_w, conv_b, conv_b)


def _pool_kernel(x_ref, g_ref, w_ref, sc_ref, gn_ref, o_ref, on_ref, e_ref, p_ref, *, bm, tiles_per_seq, group_dim):
    step = pl.program_id(0)
    tile_in_seq = step % tiles_per_seq
    halo = POOL_HALO
    rows = halo + bm

    @pl.when(step == 0)
    def _():
        e_ref[0:halo, :] = jnp.zeros((halo, e_ref.shape[1]), F32)
        p_ref[0:SUBLANE, :] = jnp.zeros((SUBLANE, p_ref.shape[1]), F32)

    @pl.when(jnp.logical_and(step > 0, tile_in_seq == 0))
    def _():
        e_ref[0:halo, :] = jnp.zeros((halo, e_ref.shape[1]), F32)

    @pl.when(tile_in_seq > 0)
    def _():
        e_ref[0:halo, :] = e_ref[bm:bm + halo, :]

    x = x_ref[...]
    xn = x * lax.rsqrt(jnp.mean(x * x, axis=-1, keepdims=True) + RMS_EPS) * g_ref[...]
    e_ref[halo:rows, :] = xn

    pos = tile_in_seq * bm + lax.broadcasted_iota(jnp.int32, (bm, 1), 0)
    lo = SUBLANE
    for gi, window in enumerate(POOL_WINDOWS):
        cs = slice(gi * group_dim, (gi + 1) * group_dim)
        src, dst, span = e_ref, p_ref, 1
        while span < window:
            dst[lo:rows, cs] = src[lo:rows, cs] + src[lo - span:rows - span, cs]
            src, dst = dst, (e_ref if dst is p_ref else p_ref)
            span *= 2
        wsum = src[halo:rows, cs]
        cnt = jnp.minimum(pos + 1, window).astype(F32)
        pooled = wsum / cnt - xn[:, cs]
        y = jnp.dot(pooled.astype(BF16), w_ref[gi], preferred_element_type=F32)
        o_ref[:, cs] = x[:, cs] + y * sc_ref[:, cs]
        if window > 2:
            e_ref[halo:rows, cs] = xn[:, cs]

    y = o_ref[...]
    on_ref[...] = (y * lax.rsqrt(jnp.mean(y * y, axis=-1, keepdims=True) + RMS_EPS) * gn_ref[...]).astype(on_ref.dtype)


def pool_mixer(x2d, norm_g, w_group, scale, next_norm_g, seq, bm_pref=256):
    t, d = x2d.shape
    groups, group_dim, _ = w_group.shape
    bm = _tile(seq, bm_pref)
    return pl.pallas_call(
        functools.partial(_pool_kernel, bm=bm, tiles_per_seq=seq // bm, group_dim=group_dim),
        grid=(t // bm,),
        in_specs=[
            pl.BlockSpec((bm, d), lambda i: (i, 0)),
            pl.BlockSpec((1, d), lambda i: (0, 0)),
            pl.BlockSpec((groups, group_dim, group_dim), lambda i: (0, 0, 0)),
            pl.BlockSpec((1, d), lambda i: (0, 0)),
            pl.BlockSpec((1, d), lambda i: (0, 0)),
        ],
        out_specs=[pl.BlockSpec((bm, d), lambda i: (i, 0)), pl.BlockSpec((bm, d), lambda i: (i, 0))],
        out_shape=[jax.ShapeDtypeStruct((t, d), F32), jax.ShapeDtypeStruct((t, d), BF16)],
        scratch_shapes=[pltpu.VMEM((POOL_HALO + bm, d), F32), pltpu.VMEM((POOL_HALO + bm, d), F32)],
        compiler_params=_params(("arbitrary",), 56),
        name="pool_mixer",
    )(x2d, norm_g.reshape(1, d), w_group, scale.reshape(1, d), next_norm_g.reshape(1, d))


def _prep_w_in_kernel(w_ref, o_ref, *, d_ff, fp):
    rows = w_ref.shape[0]
    full = (d_ff // LANE) * LANE
    tail = d_ff - full
    x = w_ref[...]
    o_ref[:, 0:full] = x[:, 0:full].astype(BF16)
    if tail == 0:
        up = x[:, d_ff:2 * d_ff]
    else:
        keep = lax.broadcasted_iota(jnp.int32, (rows, LANE), 1) < tail
        o_ref[:, full:full + LANE] = jnp.where(keep, x[:, full:full + LANE], 0.0).astype(BF16)
        src = x[:, full:2 * d_ff]
        up = pltpu.roll(src, src.shape[1] - tail, axis=1)
        o_ref[:, fp + full:fp + full + LANE] = jnp.where(keep, up[:, full:full + LANE], 0.0).astype(BF16)
    o_ref[:, fp:fp + full] = up[:, 0:full].astype(BF16)
    used = full + (LANE if tail else 0)
    if used < fp:
        zeros = jnp.zeros((rows, fp - used), BF16)
        o_ref[:, used:fp] = zeros
        o_ref[:, fp + used:2 * fp] = zeros


def prep_w_in(w_in_all, layer, fp):
    _, d, two_ff = w_in_all.shape
    d_ff = two_ff // 2
    rows = _tile(d, 128)
    return pl.pallas_call(
        functools.partial(_prep_w_in_kernel, d_ff=d_ff, fp=fp),
        grid=(d // rows,),
        in_specs=[pl.BlockSpec((None, rows, two_ff), lambda i: (layer, i, 0))],
        out_specs=pl.BlockSpec((rows, 2 * fp), lambda i: (i, 0)),
        out_shape=jax.ShapeDtypeStruct((d, 2 * fp), BF16),
        compiler_params=_params(("parallel",), 48),
        name="prep_w_in",
    )(w_in_all)


def _prep_w_out_kernel(w_ref, o_ref, *, n_src):
    @pl.when(pl.program_id(0) < n_src)
    def _():
        o_ref[...] = w_ref[...].astype(BF16)

    @pl.when(pl.program_id(0) >= n_src)
    def _():
        o_ref[...] = jnp.zeros_like(o_ref)


def prep_w_out(w_out_all, layer, fp):
    _, d_ff, d = w_out_all.shape
    rows = 64
    assert d_ff % rows == 0 and fp % rows == 0
    n_src = d_ff // rows
    return pl.pallas_call(
        functools.partial(_prep_w_out_kernel, n_src=n_src),
        grid=(fp // rows,),
        in_specs=[pl.BlockSpec((None, rows, d), lambda i: (layer, jnp.minimum(i, n_src - 1), 0))],
        out_specs=pl.BlockSpec((rows, d), lambda i: (i, 0)),
        out_shape=jax.ShapeDtypeStruct((fp, d), BF16),
        compiler_params=_params(("parallel",), 32),
        name="prep_w_out",
    )(w_out_all)


def _pad_cols(w, width):
    return jnp.pad(w, [(0, 0)] * (w.ndim - 1) + [(0, width - w.shape[-1])])


FFN_OUT_K_STEPS = 4


def _padded_hidden(d_ff):
    unit = 1024 if d_ff > 1024 else LANE * FFN_OUT_K_STEPS
    return -(-d_ff // unit) * unit


def _conv_ffn(x2d, xn, w_in_all, conv_w, conv_b, w_out_all, layer, seq):
    d_ff = w_out_all.shape[1]
    fp = _padded_hidden(d_ff)
    w_in_p = prep_w_in(w_in_all, layer, fp)
    w_out_p = prep_w_out(w_out_all, layer, fp)
    conv_w_p = _pad_cols(conv_w.reshape(FFN_CONV_WIDTH, 2, d_ff), fp).reshape(FFN_CONV_WIDTH, 2 * fp)
    conv_b_p = _pad_cols(conv_b.reshape(1, 2, d_ff), fp).reshape(1, 2 * fp)
    act = ffn_in(xn, w_in_p, conv_w_p, conv_b_p, seq)
    return matmul_residual(act, w_out_p, x2d, FFN_OUT_K_STEPS)


def _mlstm_layer(x2d, norm_g, w_in, b_i, b_f, head_g, w_out, batch, seq):
    heads = b_i.shape[0]
    main = w_in.shape[1] - 2 * heads
    xn = rmsnorm(x2d, norm_g, BF16)
    proj = matmul(xn, w_in.astype(BF16), BF16, n=main)
    gates = matmul(xn, _pad_cols(w_in[:, main:], LANE).astype(BF16), F32)
    hgated = mlstm_recurrence(proj, gates, b_i, b_f, head_g, batch, seq)
    return matmul_residual(hgated, w_out.astype(BF16), x2d, 1)


def kernel(x, mlstm_norm_g, mlstm_w_in, mlstm_b_i, mlstm_b_f, mlstm_head_g, mlstm_w_out,
           pool_norm_g, pool_w_group, pool_scale,
           ffn_norm_g, ffn_w_in, ffn_conv_w, ffn_conv_b, ffn_w_out, final_norm_g):
    batch, seq, d = x.shape
    depth = ffn_w_in.shape[0]
    x2d = x.reshape(batch * seq, d)
    for layer in range(depth):
        j = layer // 2
        if layer % 2 == 0:
            x2d = _mlstm_layer(x2d, mlstm_norm_g[j], mlstm_w_in[j], mlstm_b_i[j], mlstm_b_f[j],
                               mlstm_head_g[j], mlstm_w_out[j], batch, seq)
            xn = rmsnorm(x2d, ffn_norm_g[layer], BF16)
        else:
            x2d, xn = pool_mixer(x2d, pool_norm_g[j], pool_w_group[j].astype(BF16), pool_scale[j],
                                 ffn_norm_g[layer], seq)
        x2d = _conv_ffn(x2d, xn, ffn_w_in, ffn_conv_w[layer], ffn_conv_b[layer], ffn_w_out, layer, seq)
    return rmsnorm(x2d, final_norm_g, x.dtype).reshape(batch, seq, d)
```

```python
import functools

import jax
import jax.numpy as jnp
from jax import lax
from jax.experimental import pallas as pl
from jax.experimental.pallas import tpu as pltpu

RMS_EPS = 1e-6
GATE_SOFTCAP = 15.0
POOL_WINDOWS = (2, 4, 8, 16)
FFN_CONV_WIDTH = 3
MLSTM_CHUNK = 256
MLSTM_HEADS_PER_STEP = 2
LANE = 128
SUBLANE = 8
POOL_HALO = 32
MIB = 1024 * 1024

F32 = jnp.float32
BF16 = jnp.bfloat16


def _params(semantics, vmem_mib):
    return pltpu.CompilerParams(dimension_semantics=semantics, vmem_limit_bytes=vmem_mib * MIB)


def _tile(dim, pref):
    t = min(dim, pref)
    while dim % t:
        t //= 2
    return t


def _rmsnorm_kernel(x_ref, g_ref, o_ref):
    x = x_ref[...]
    y = x * lax.rsqrt(jnp.mean(x * x, axis=-1, keepdims=True) + RMS_EPS)
    o_ref[...] = (y * g_ref[...]).astype(o_ref.dtype)


def rmsnorm(x2d, g, out_dtype):
    t, d = x2d.shape
    bm = _tile(t, 256)
    return pl.pallas_call(
        _rmsnorm_kernel,
        grid=(t // bm,),
        in_specs=[pl.BlockSpec((bm, d), lambda i: (i, 0)), pl.BlockSpec((1, d), lambda i: (0, 0))],
        out_specs=pl.BlockSpec((bm, d), lambda i: (i, 0)),
        out_shape=jax.ShapeDtypeStruct((t, d), out_dtype),
        compiler_params=_params(("parallel",), 32),
        name="rmsnorm",
    )(x2d, g.reshape(1, d))


def _mm_kernel(a_ref, w_ref, o_ref):
    o_ref[...] = jnp.dot(a_ref[...], w_ref[...], preferred_element_type=F32).astype(o_ref.dtype)


def _mm_cast_kernel(a_ref, w_ref, o_ref, wb_ref):
    @pl.when(pl.program_id(1) == 0)
    def _():
        wb_ref[...] = w_ref[...].astype(BF16)

    o_ref[...] = jnp.dot(a_ref[...], wb_ref[...], preferred_element_type=F32).astype(o_ref.dtype)


def matmul(a, w, out_dtype, n=None, bm_pref=1024, bn_pref=1024):
    m, k = a.shape
    n = w.shape[1] if n is None else n
    if w.dtype == F32:
        bm, bn = _tile(m, bm_pref), _tile(n, bn_pref // 2)
        return pl.pallas_call(
            _mm_cast_kernel,
            grid=(n // bn, m // bm),
            in_specs=[pl.BlockSpec((bm, k), lambda j, i: (i, 0)), pl.BlockSpec((k, bn), lambda j, i: (0, j))],
            out_specs=pl.BlockSpec((bm, bn), lambda j, i: (i, j)),
            out_shape=jax.ShapeDtypeStruct((m, n), out_dtype),
            scratch_shapes=[pltpu.VMEM((k, bn), BF16)],
            compiler_params=_params(("parallel", "arbitrary"), 56),
            name="matmul_f32w",
        )(a, w)
    bm, bn = _tile(m, bm_pref), _tile(n, bn_pref)
    return pl.pallas_call(
        _mm_kernel,
        grid=(n // bn, m // bm),
        in_specs=[pl.BlockSpec((bm, k), lambda j, i: (i, 0)), pl.BlockSpec((k, bn), lambda j, i: (0, j))],
        out_specs=pl.BlockSpec((bm, bn), lambda j, i: (i, j)),
        out_shape=jax.ShapeDtypeStruct((m, n), out_dtype),
        compiler_params=_params(("parallel", "parallel"), 56),
        name="matmul",
    )(a, w)


def _mm_res_kernel(a_ref, w_ref, x_ref, o_ref, *, nk):
    def partial_product():
        return jnp.dot(a_ref[...], w_ref[...], preferred_element_type=F32)

    if nk == 1:
        o_ref[...] = x_ref[...] + partial_product()
        return
    kk = pl.program_id(2)

    @pl.when(kk == 0)
    def _():
        o_ref[...] = partial_product()

    @pl.when(jnp.logical_and(kk > 0, kk < nk - 1))
    def _():
        o_ref[...] += partial_product()

    @pl.when(kk == nk - 1)
    def _():
        o_ref[...] = x_ref[...] + (o_ref[...] + partial_product())


def matmul_residual(a, w, x, nk, bm_pref=1024, bn_pref=1024):
    m, k = a.shape
    n = w.shape[1]
    bm, bn, bk = _tile(m, bm_pref), _tile(n, bn_pref), k // nk
    assert bk * nk == k and (nk == 1 or bk % LANE == 0)
    return pl.pallas_call(
        functools.partial(_mm_res_kernel, nk=nk),
        grid=(m // bm, n // bn, nk),
        in_specs=[
            pl.BlockSpec((bm, bk), lambda i, j, kk: (i, kk)),
            pl.BlockSpec((bk, bn), lambda i, j, kk: (kk, j)),
            pl.BlockSpec((bm, bn), lambda i, j, kk: (i, j)),
        ],
        out_specs=pl.BlockSpec((bm, bn), lambda i, j, kk: (i, j)),
        out_shape=jax.ShapeDtypeStruct((m, n), F32),
        compiler_params=_params(("parallel", "parallel", "arbitrary"), 56),
        name="matmul_residual",
    )(a, w, x)


def _as_column(row, length):
    return jnp.transpose(jnp.broadcast_to(row, (LANE, length)))[:, 0:1]


def _cumsum_lanes(row, tri):
    x = jnp.broadcast_to(row, (SUBLANE, row.shape[1]))
    hi = x.astype(BF16).astype(F32)
    r1 = x - hi
    mid = r1.astype(BF16).astype(F32)
    lo = (r1 - mid).astype(BF16).astype(F32)
    parts = jnp.concatenate([hi, mid, lo], axis=0).astype(BF16)
    acc = jnp.dot(parts, tri, preferred_element_type=F32)
    s = acc[0:SUBLANE] + acc[SUBLANE:2 * SUBLANE] + acc[2 * SUBLANE:3 * SUBLANE]
    return s[0:1, :]


def _mlstm_kernel(bi_ref, bf_ref, q_ref, k_ref, v_ref, o_ref, gf_ref, hg_ref, tri_ref, out_ref,
                  c_ref, n_ref, m_ref, *, chunk, dqk, dv, heads_per_step, qk_scale):
    @pl.when(pl.program_id(2) == 0)
    def _():
        c_ref[...] = jnp.zeros_like(c_ref)
        n_ref[...] = jnp.zeros_like(n_ref)
        m_ref[...] = jnp.zeros_like(m_ref)

    tri = tri_ref[...]
    t_idx = lax.broadcasted_iota(jnp.int32, (chunk, chunk), 0)
    s_idx = lax.broadcasted_iota(jnp.int32, (chunk, chunk), 1)
    causal = t_idx >= s_idx

    for hh in range(heads_per_step):
        h = pl.program_id(1) * heads_per_step + hh
        qs = slice(hh * dqk, (hh + 1) * dqk)
        vs = slice(hh * dv, (hh + 1) * dv)
        gf = gf_ref[hh]
        i_row = GATE_SOFTCAP * jnp.tanh((gf[0:1, :] + bi_ref[h]) / GATE_SOFTCAP)
        logf = jax.nn.log_sigmoid(GATE_SOFTCAP * jnp.tanh((gf[1:2, :] + bf_ref[h]) / GATE_SOFTCAP))
        b_row = _cumsum_lanes(logf, tri)
        d_row = i_row - b_row
        b_col = _as_column(b_row, chunk)
        d_col = _as_column(d_row, chunk)

        log_intra = jnp.where(causal, b_col + d_row, -jnp.inf)
        m_prev = m_ref[hh, 0:1, 0:1]
        log_inter = b_col + m_prev
        m_t = jnp.maximum(log_inter, jnp.max(log_intra, axis=1, keepdims=True))
        w_inter = jnp.exp(log_inter - m_t)
        w_intra = jnp.exp(log_intra - m_t)

        q = q_ref[:, qs] * qk_scale
        k = k_ref[:, qs]
        v = v_ref[:, vs]
        s = lax.dot_general(q, k, (((1,), (1,)), ((), ())), preferred_element_type=F32) * w_intra
        c_prev = c_ref[hh]
        n_prev = n_ref[hh]
        num = (w_inter * jnp.dot(q, c_prev.astype(BF16), preferred_element_type=F32)
               + jnp.dot(s.astype(BF16), v, preferred_element_type=F32))
        qn = jnp.sum(q.astype(F32) * n_prev, axis=1, keepdims=True)
        den = w_inter * qn + jnp.sum(s, axis=1, keepdims=True)
        hval = num / jnp.maximum(jnp.abs(den), jnp.exp(-m_t))
        hval = hval * lax.rsqrt(jnp.mean(hval * hval, axis=-1, keepdims=True) + RMS_EPS)
        hval = hval * hg_ref[:, vs]
        out_ref[:, vs] = (hval * jax.nn.sigmoid(o_ref[:, vs].astype(F32))).astype(out_ref.dtype)

        b_last = b_row[:, chunk - 1:chunk]
        log_prev = b_last + m_prev
        m_new = jnp.maximum(log_prev, jnp.max(b_last + d_row, axis=1, keepdims=True))
        a_prev = jnp.exp(log_prev - m_new)
        ka = jnp.exp(b_last + d_col - m_new) * k.astype(F32)
        c_ref[hh] = a_prev * c_prev + lax.dot_general(
            ka.astype(BF16), v, (((0,), (0,)), ((), ())), preferred_element_type=F32)
        n_ref[hh] = a_prev * n_prev + jnp.sum(ka, axis=0, keepdims=True)
        m_ref[hh] = jnp.broadcast_to(m_new, m_ref.shape[1:])


def mlstm_recurrence(proj, gates, b_i, b_f, head_g, batch, seq):
    heads = b_i.shape[0]
    dv = head_g.shape[0] // heads
    dqk = dv // 2
    chunk = _tile(seq, MLSTM_CHUNK)
    nc = seq // chunk
    hps = _tile(heads, MLSTM_HEADS_PER_STEP)
    proj3 = proj.reshape(batch, seq, proj.shape[1])
    gf = gates[:, :2 * heads].reshape(batch, nc, chunk, 2, heads).transpose(0, 4, 1, 3, 2)
    tri = (jnp.arange(chunk)[:, None] <= jnp.arange(chunk)[None, :]).astype(BF16)
    k_off = heads // hps
    v_off = 2 * heads * dqk // (dv * hps)
    o_off = v_off + heads // hps
    smem = pl.BlockSpec(memory_space=pltpu.SMEM)
    out = pl.pallas_call(
        functools.partial(_mlstm_kernel, chunk=chunk, dqk=dqk, dv=dv, heads_per_step=hps,
                          qk_scale=float(dqk) ** -0.5),
        grid=(batch, heads // hps, nc),
        in_specs=[
            smem, smem,
            pl.BlockSpec((None, chunk, hps * dqk), lambda b, h, c: (b, c, h)),
            pl.BlockSpec((None, chunk, hps * dqk), lambda b, h, c: (b, c, k_off + h)),
            pl.BlockSpec((None, chunk, hps * dv), lambda b, h, c: (b, c, v_off + h)),
            pl.BlockSpec((None, chunk, hps * dv), lambda b, h, c: (b, c, o_off + h)),
            pl.BlockSpec((None, hps, None, 2, chunk), lambda b, h, c: (b, h, c, 0, 0)),
            pl.BlockSpec((1, hps * dv), lambda b, h, c: (0, h)),
            pl.BlockSpec((chunk, chunk), lambda b, h, c: (0, 0)),
        ],
        out_specs=pl.BlockSpec((None, chunk, hps * dv), lambda b, h, c: (b, c, h)),
        out_shape=jax.ShapeDtypeStruct((batch, seq, heads * dv), BF16),
        scratch_shapes=[
            pltpu.VMEM((hps, dqk, dv), F32),
            pltpu.VMEM((hps, 1, dqk), F32),
            pltpu.VMEM((hps, SUBLANE, LANE), F32),
        ],
        compiler_params=_params(("parallel", "parallel", "arbitrary"), 48),
        name="mlstm_recurrence",
    )(b_i, b_f, proj3, proj3, proj3, proj3, gf, head_g.reshape(1, heads * dv), tri)
    return out.reshape(batch * seq, heads * dv)


def _ffn_in_kernel(a_ref, wg_ref, wu_ref, cwg_ref, cwu_ref, cbg_ref, cbu_ref, o_ref, hg_ref, hu_ref,
                   *, bm, tiles_per_seq):
    first = (pl.program_id(1) % tiles_per_seq) == 0

    @pl.when(first)
    def _():
        hg_ref[0:SUBLANE, :] = jnp.zeros((SUBLANE, hg_ref.shape[1]), F32)
        hu_ref[0:SUBLANE, :] = jnp.zeros((SUBLANE, hu_ref.shape[1]), F32)

    @pl.when(jnp.logical_not(first))
    def _():
        hg_ref[0:SUBLANE, :] = hg_ref[bm:bm + SUBLANE, :]
        hu_ref[0:SUBLANE, :] = hu_ref[bm:bm + SUBLANE, :]

    a = a_ref[...]

    def conv_half(w_ref, cw_ref, cb_ref, h_ref):
        hcur = jnp.dot(a, w_ref[...], preferred_element_type=F32)
        h_ref[SUBLANE:SUBLANE + bm, :] = hcur
        hall = h_ref[0:SUBLANE + bm, :]
        h1 = pltpu.roll(hall, 1, axis=0)[SUBLANE:, :]
        h2 = pltpu.roll(hall, 2, axis=0)[SUBLANE:, :]
        cw = cw_ref[...]
        return cb_ref[...] + cw[0:1, :] * h2 + cw[1:2, :] * h1 + cw[2:3, :] * hcur

    g = conv_half(wg_ref, cwg_ref, cbg_ref, hg_ref)
    u = conv_half(wu_ref, cwu_ref, cbu_ref, hu_ref)
    o_ref[...] = (jax.nn.silu(g) * u).astype(o_ref.dtype)


def ffn_in(a, w_in, conv_w, conv_b, seq, bm_pref=1024, bn_pref=512):
    t, d = a.shape
    fp = w_in.shape[1] // 2
    bm, bn = _tile(seq, bm_pref), _tile(fp, bn_pref)
    nj = fp // bn
    return pl.pallas_call(
        functools.partial(_ffn_in_kernel, bm=bm, tiles_per_seq=seq // bm),
        grid=(nj, t // bm),
        in_specs=[
            pl.BlockSpec((bm, d), lambda j, i: (i, 0)),
            pl.BlockSpec((d, bn), lambda j, i: (0, j)),
            pl.BlockSpec((d, bn), lambda j, i: (0, nj + j)),
            pl.BlockSpec((FFN_CONV_WIDTH, bn), lambda j, i: (0, j)),
            pl.BlockSpec((FFN_CONV_WIDTH, bn), lambda j, i: (0, nj + j)),
            pl.BlockSpec((1, bn), lambda j, i: (0, j)),
            pl.BlockSpec((1, bn), lambda j, i: (0, nj + j)),
        ],
        out_specs=pl.BlockSpec((bm, bn), lambda j, i: (i, j)),
        out_shape=jax.ShapeDtypeStruct((t, fp), BF16),
        scratch_shapes=[pltpu.VMEM((bm + 2 * SUBLANE, bn), F32), pltpu.VMEM((bm + 2 * SUBLANE, bn), F32)],
        compiler_params=_params(("parallel", "arbitrary"), 56),
        name="ffn_in",
    )(a, w_in, w_in, conv_w, conv_w, conv_b, conv_b)


def _pool_kernel(x_ref, g_ref, w_ref, sc_ref, gn_ref, o_ref, on_ref, e_ref, p_ref, *, bm, tiles_per_seq, group_dim):
    step = pl.program_id(0)
    tile_in_seq = step % tiles_per_seq
    halo = POOL_HALO
    rows = halo + bm

    @pl.when(step == 0)
    def _():
        e_ref[0:halo, :] = jnp.zeros((halo, e_ref.shape[1]), F32)
        p_ref[0:SUBLANE, :] = jnp.zeros((SUBLANE, p_ref.shape[1]), F32)

    @pl.when(jnp.logical_and(step > 0, tile_in_seq == 0))
    def _():
        e_ref[0:halo, :] = jnp.zeros((halo, e_ref.shape[1]), F32)

    @pl.when(tile_in_seq > 0)
    def _():
        e_ref[0:halo, :] = e_ref[bm:bm + halo, :]

    x = x_ref[...]
    xn = x * lax.rsqrt(jnp.mean(x * x, axis=-1, keepdims=True) + RMS_EPS) * g_ref[...]
    e_ref[halo:rows, :] = xn

    pos = tile_in_seq * bm + lax.broadcasted_iota(jnp.int32, (bm, 1), 0)
    lo = SUBLANE
    for gi, window in enumerate(POOL_WINDOWS):
        cs = slice(gi * group_dim, (gi + 1) * group_dim)
        src, dst, span = e_ref, p_ref, 1
        while span < window:
            dst[lo:rows, cs] = src[lo:rows, cs] + src[lo - span:rows - span, cs]
            src, dst = dst, (e_ref if dst is p_ref else p_ref)
            span *= 2
        wsum = src[halo:rows, cs]
        cnt = jnp.minimum(pos + 1, window).astype(F32)
        pooled = wsum / cnt - xn[:, cs]
        y = jnp.dot(pooled.astype(BF16), w_ref[gi], preferred_element_type=F32)
        o_ref[:, cs] = x[:, cs] + y * sc_ref[:, cs]
        if window > 2:
            e_ref[halo:rows, cs] = xn[:, cs]

    y = o_ref[...]
    on_ref[...] = (y * lax.rsqrt(jnp.mean(y * y, axis=-1, keepdims=True) + RMS_EPS) * gn_ref[...]).astype(on_ref.dtype)


def pool_mixer(x2d, norm_g, w_group, scale, next_norm_g, seq, bm_pref=256):
    t, d = x2d.shape
    groups, group_dim, _ = w_group.shape
    bm = _tile(seq, bm_pref)
    return pl.pallas_call(
        functools.partial(_pool_kernel, bm=bm, tiles_per_seq=seq // bm, group_dim=group_dim),
        grid=(t // bm,),
        in_specs=[
            pl.BlockSpec((bm, d), lambda i: (i, 0)),
            pl.BlockSpec((1, d), lambda i: (0, 0)),
            pl.BlockSpec((groups, group_dim, group_dim), lambda i: (0, 0, 0)),
            pl.BlockSpec((1, d), lambda i: (0, 0)),
            pl.BlockSpec((1, d), lambda i: (0, 0)),
        ],
        out_specs=[pl.BlockSpec((bm, d), lambda i: (i, 0)), pl.BlockSpec((bm, d), lambda i: (i, 0))],
        out_shape=[jax.ShapeDtypeStruct((t, d), F32), jax.ShapeDtypeStruct((t, d), BF16)],
        scratch_shapes=[pltpu.VMEM((POOL_HALO + bm, d), F32), pltpu.VMEM((POOL_HALO + bm, d), F32)],
        compiler_params=_params(("arbitrary",), 56),
        name="pool_mixer",
    )(x2d, norm_g.reshape(1, d), w_group, scale.reshape(1, d), next_norm_g.reshape(1, d))


def _prep_w_in_kernel(w_ref, o_ref, *, d_ff, fp):
    rows = w_ref.shape[0]
    full = (d_ff // LANE) * LANE
    tail = d_ff - full
    x = w_ref[...]
    o_ref[:, 0:full] = x[:, 0:full].astype(BF16)
    if tail == 0:
        up = x[:, d_ff:2 * d_ff]
    else:
        keep = lax.broadcasted_iota(jnp.int32, (rows, LANE), 1) < tail
        o_ref[:, full:full + LANE] = jnp.where(keep, x[:, full:full + LANE], 0.0).astype(BF16)
        src = x[:, full:2 * d_ff]
        up = pltpu.roll(src, src.shape[1] - tail, axis=1)
        o_ref[:, fp + full:fp + full + LANE] = jnp.where(keep, up[:, full:full + LANE], 0.0).astype(BF16)
    o_ref[:, fp:fp + full] = up[:, 0:full].astype(BF16)
    used = full + (LANE if tail else 0)
    if used < fp:
        zeros = jnp.zeros((rows, fp - used), BF16)
        o_ref[:, used:fp] = zeros
        o_ref[:, fp + used:2 * fp] = zeros


def prep_w_in(w_in_all, layer, fp):
    _, d, two_ff = w_in_all.shape
    d_ff = two_ff // 2
    rows = _tile(d, 128)
    return pl.pallas_call(
        functools.partial(_prep_w_in_kernel, d_ff=d_ff, fp=fp),
        grid=(d // rows,),
        in_specs=[pl.BlockSpec((None, rows, two_ff), lambda i: (layer, i, 0))],
        out_specs=pl.BlockSpec((rows, 2 * fp), lambda i: (i, 0)),
        out_shape=jax.ShapeDtypeStruct((d, 2 * fp), BF16),
        compiler_params=_params(("parallel",), 48),
        name="prep_w_in",
    )(w_in_all)


def _prep_w_out_kernel(w_ref, o_ref, *, n_src):
    @pl.when(pl.program_id(0) < n_src)
    def _():
        o_ref[...] = w_ref[...].astype(BF16)

    @pl.when(pl.program_id(0) >= n_src)
    def _():
        o_ref[...] = jnp.zeros_like(o_ref)


def prep_w_out(w_out_all, layer, fp):
    _, d_ff, d = w_out_all.shape
    unit = 64
    assert d_ff % unit == 0
    units = d_ff // unit
    rows = unit * max(c for c in range(1, units + 1)
                      if units % c == 0 and c * unit * d * 4 <= 10 * MIB)
    n_src = d_ff // rows
    n_blocks = -(-fp // rows)
    return pl.pallas_call(
        functools.partial(_prep_w_out_kernel, n_src=n_src),
        grid=(n_blocks,),
        in_specs=[pl.BlockSpec((None, rows, d), lambda i: (layer, jnp.minimum(i, n_src - 1), 0))],
        out_specs=pl.BlockSpec((rows, d), lambda i: (i, 0)),
        out_shape=jax.ShapeDtypeStruct((n_blocks * rows, d), BF16),
        compiler_params=_params(("parallel",), 40),
        name="prep_w_out",
    )(w_out_all)


def _pad_cols(w, width):
    return jnp.pad(w, [(0, 0)] * (w.ndim - 1) + [(0, width - w.shape[-1])])


FFN_OUT_K_STEPS = 4


def _padded_hidden(d_ff):
    unit = 1024 if d_ff > 1024 else LANE * FFN_OUT_K_STEPS
    return -(-d_ff // unit) * unit


def _conv_ffn(x2d, xn, w_in_all, conv_w, conv_b, w_out_all, layer, seq):
    d_ff = w_out_all.shape[1]
    fp = _padded_hidden(d_ff)
    w_in_p = prep_w_in(w_in_all, layer, fp)
    w_out_p = prep_w_out(w_out_all, layer, fp)
    conv_w_p = _pad_cols(conv_w.reshape(FFN_CONV_WIDTH, 2, d_ff), fp).reshape(FFN_CONV_WIDTH, 2 * fp)
    conv_b_p = _pad_cols(conv_b.reshape(1, 2, d_ff), fp).reshape(1, 2 * fp)
    act = ffn_in(xn, w_in_p, conv_w_p, conv_b_p, seq)
    return matmul_residual(act, w_out_p, x2d, FFN_OUT_K_STEPS)


def _mlstm_layer(x2d, norm_g, w_in, b_i, b_f, head_g, w_out, batch, seq):
    heads = b_i.shape[0]
    main = w_in.shape[1] - 2 * heads
    xn = rmsnorm(x2d, norm_g, BF16)
    proj = matmul(xn, w_in, BF16, n=main)
    gates = matmul(xn, _pad_cols(w_in[:, main:], LANE).astype(BF16), F32)
    hgated = mlstm_recurrence(proj, gates, b_i, b_f, head_g, batch, seq)
    return matmul_residual(hgated, w_out.astype(BF16), x2d, 1)


def kernel(x, mlstm_norm_g, mlstm_w_in, mlstm_b_i, mlstm_b_f, mlstm_head_g, mlstm_w_out,
           pool_norm_g, pool_w_group, pool_scale,
           ffn_norm_g, ffn_w_in, ffn_conv_w, ffn_conv_b, ffn_w_out, final_norm_g):
    batch, seq, d = x.shape
    depth = ffn_w_in.shape[0]
    x2d = x.reshape(batch * seq, d)
    for layer in range(depth):
        j = layer // 2
        if layer % 2 == 0:
            x2d = _mlstm_layer(x2d, mlstm_norm_g[j], mlstm_w_in[j], mlstm_b_i[j], mlstm_b_f[j],
                               mlstm_head_g[j], mlstm_w_out[j], batch, seq)
            xn = rmsnorm(x2d, ffn_norm_g[layer], BF16)
        else:
            x2d, xn = pool_mixer(x2d, pool_norm_g[j], pool_w_group[j].astype(BF16), pool_scale[j],
                                 ffn_norm_g[layer], seq)
        x2d = _conv_ffn(x2d, xn, ffn_w_in, ffn_conv_w[layer], ffn_conv_b[layer], ffn_w_out, layer, seq)
    return rmsnorm(x2d, final_norm_g, x.dtype).reshape(batch, seq, d)
```

```python
import functools

import jax
import jax.numpy as jnp
from jax import lax
from jax.experimental import pallas as pl
from jax.experimental.pallas import tpu as pltpu

RMS_EPS = 1e-6
GATE_SOFTCAP = 15.0
POOL_WINDOWS = (2, 4, 8, 16)
FFN_CONV_WIDTH = 3
MLSTM_CHUNK = 256
MLSTM_HEADS_PER_STEP = 2
LANE = 128
SUBLANE = 8
POOL_HALO = 32
MIB = 1024 * 1024

F32 = jnp.float32
BF16 = jnp.bfloat16


def _params(semantics, vmem_mib):
    return pltpu.CompilerParams(dimension_semantics=semantics, vmem_limit_bytes=vmem_mib * MIB)


def _tile(dim, pref):
    t = min(dim, pref)
    while dim % t:
        t //= 2
    return t


def _rmsnorm_kernel(x_ref, g_ref, o_ref):
    x = x_ref[...]
    y = x * lax.rsqrt(jnp.mean(x * x, axis=-1, keepdims=True) + RMS_EPS)
    o_ref[...] = (y * g_ref[...]).astype(o_ref.dtype)


def rmsnorm(x2d, g, out_dtype):
    t, d = x2d.shape
    bm = _tile(t, 256)
    return pl.pallas_call(
        _rmsnorm_kernel,
        grid=(t // bm,),
        in_specs=[pl.BlockSpec((bm, d), lambda i: (i, 0)), pl.BlockSpec((1, d), lambda i: (0, 0))],
        out_specs=pl.BlockSpec((bm, d), lambda i: (i, 0)),
        out_shape=jax.ShapeDtypeStruct((t, d), out_dtype),
        compiler_params=_params(("parallel",), 32),
        name="rmsnorm",
    )(x2d, g.reshape(1, d))


def _cast_kernel(w_ref, o_ref):
    o_ref[...] = w_ref[...].astype(o_ref.dtype)


def cast_layer(w_all, layer, out_dtype=BF16):
    _, r, c = w_all.shape
    rows = 16
    while r % (rows * 2) == 0 and rows * 2 * c * 4 <= 8 * MIB:
        rows *= 2
    return pl.pallas_call(
        _cast_kernel,
        grid=(r // rows,),
        in_specs=[pl.BlockSpec((None, rows, c), lambda i: (layer, i, 0))],
        out_specs=pl.BlockSpec((rows, c), lambda i: (i, 0)),
        out_shape=jax.ShapeDtypeStruct((r, c), out_dtype),
        compiler_params=_params(("parallel",), 40),
        name="cast_layer",
    )(w_all)


def _mm_kernel(a_ref, w_ref, o_ref):
    o_ref[...] = jnp.dot(a_ref[...], w_ref[...], preferred_element_type=F32).astype(o_ref.dtype)


def matmul(a, w, out_dtype, n=None, bm_pref=1024, bn_pref=1024):
    m, k = a.shape
    n = w.shape[1] if n is None else n
    bm, bn = _tile(m, bm_pref), _tile(n, bn_pref)
    return pl.pallas_call(
        _mm_kernel,
        grid=(n // bn, m // bm),
        in_specs=[pl.BlockSpec((bm, k), lambda j, i: (i, 0)), pl.BlockSpec((k, bn), lambda j, i: (0, j))],
        out_specs=pl.BlockSpec((bm, bn), lambda j, i: (i, j)),
        out_shape=jax.ShapeDtypeStruct((m, n), out_dtype),
        compiler_params=_params(("parallel", "parallel"), 56),
        name="matmul",
    )(a, w)


def _mm_res_kernel(a_ref, w_ref, x_ref, o_ref, *, nk):
    def partial_product():
        return jnp.dot(a_ref[...], w_ref[...], preferred_element_type=F32)

    if nk == 1:
        o_ref[...] = x_ref[...] + partial_product()
        return
    kk = pl.program_id(2)

    @pl.when(kk == 0)
    def _():
        o_ref[...] = partial_product()

    @pl.when(jnp.logical_and(kk > 0, kk < nk - 1))
    def _():
        o_ref[...] += partial_product()

    @pl.when(kk == nk - 1)
    def _():
        o_ref[...] = x_ref[...] + (o_ref[...] + partial_product())


def matmul_residual(a, w, x, nk, bm_pref=1024, bn_pref=1024):
    m, k = a.shape
    n = w.shape[1]
    bm, bn, bk = _tile(m, bm_pref), _tile(n, bn_pref), k // nk
    assert bk * nk == k and (nk == 1 or bk % LANE == 0)
    return pl.pallas_call(
        functools.partial(_mm_res_kernel, nk=nk),
        grid=(m // bm, n // bn, nk),
        in_specs=[
            pl.BlockSpec((bm, bk), lambda i, j, kk: (i, kk)),
            pl.BlockSpec((bk, bn), lambda i, j, kk: (kk, j)),
            pl.BlockSpec((bm, bn), lambda i, j, kk: (i, j)),
        ],
        out_specs=pl.BlockSpec((bm, bn), lambda i, j, kk: (i, j)),
        out_shape=jax.ShapeDtypeStruct((m, n), F32),
        compiler_params=_params(("parallel", "parallel", "arbitrary"), 56),
        name="matmul_residual",
    )(a, w, x)


def _as_column(row, length):
    return jnp.transpose(jnp.broadcast_to(row, (LANE, length)))[:, 0:1]


def _cumsum_lanes(row, tri):
    x = jnp.broadcast_to(row, (SUBLANE, row.shape[1]))
    hi = x.astype(BF16).astype(F32)
    r1 = x - hi
    mid = r1.astype(BF16).astype(F32)
    lo = (r1 - mid).astype(BF16).astype(F32)
    parts = jnp.concatenate([hi, mid, lo], axis=0).astype(BF16)
    acc = jnp.dot(parts, tri, preferred_element_type=F32)
    s = acc[0:SUBLANE] + acc[SUBLANE:2 * SUBLANE] + acc[2 * SUBLANE:3 * SUBLANE]
    return s[0:1, :]


def _mlstm_kernel(bi_ref, bf_ref, q_ref, k_ref, v_ref, o_ref, gf_ref, hg_ref, tri_ref, out_ref,
                  c_ref, n_ref, m_ref, *, chunk, dqk, dv, heads_per_step, qk_scale):
    @pl.when(pl.program_id(2) == 0)
    def _():
        c_ref[...] = jnp.zeros_like(c_ref)
        n_ref[...] = jnp.zeros_like(n_ref)
        m_ref[...] = jnp.zeros_like(m_ref)

    tri = tri_ref[...]
    t_idx = lax.broadcasted_iota(jnp.int32, (chunk, chunk), 0)
    s_idx = lax.broadcasted_iota(jnp.int32, (chunk, chunk), 1)
    causal = t_idx >= s_idx

    for hh in range(heads_per_step):
        h = pl.program_id(1) * heads_per_step + hh
        qs = slice(hh * dqk, (hh + 1) * dqk)
        vs = slice(hh * dv, (hh + 1) * dv)
        gf = gf_ref[hh]
        i_row = GATE_SOFTCAP * jnp.tanh((gf[0:1, :] + bi_ref[h]) / GATE_SOFTCAP)
        logf = jax.nn.log_sigmoid(GATE_SOFTCAP * jnp.tanh((gf[1:2, :] + bf_ref[h]) / GATE_SOFTCAP))
        b_row = _cumsum_lanes(logf, tri)
        d_row = i_row - b_row
        b_col = _as_column(b_row, chunk)
        d_col = _as_column(d_row, chunk)

        log_intra = jnp.where(causal, b_col + d_row, -jnp.inf)
        m_prev = m_ref[hh, 0:1, 0:1]
        log_inter = b_col + m_prev
        m_t = jnp.maximum(log_inter, jnp.max(log_intra, axis=1, keepdims=True))
        w_inter = jnp.exp(log_inter - m_t)
        w_intra = jnp.exp(log_intra - m_t)

        q = q_ref[:, qs] * qk_scale
        k = k_ref[:, qs]
        v = v_ref[:, vs]
        s = lax.dot_general(q, k, (((1,), (1,)), ((), ())), preferred_element_type=F32) * w_intra
        c_prev = c_ref[hh]
        n_prev = n_ref[hh]
        num = (w_inter * jnp.dot(q, c_prev.astype(BF16), preferred_element_type=F32)
               + jnp.dot(s.astype(BF16), v, preferred_element_type=F32))
        qn = jnp.sum(q.astype(F32) * n_prev, axis=1, keepdims=True)
        den = w_inter * qn + jnp.sum(s, axis=1, keepdims=True)
        hval = num / jnp.maximum(jnp.abs(den), jnp.exp(-m_t))
        hval = hval * lax.rsqrt(jnp.mean(hval * hval, axis=-1, keepdims=True) + RMS_EPS)
        hval = hval * hg_ref[:, vs]
        out_ref[:, vs] = (hval * jax.nn.sigmoid(o_ref[:, vs].astype(F32))).astype(out_ref.dtype)

        b_last = b_row[:, chunk - 1:chunk]
        log_prev = b_last + m_prev
        m_new = jnp.maximum(log_prev, jnp.max(b_last + d_row, axis=1, keepdims=True))
        a_prev = jnp.exp(log_prev - m_new)
        ka = jnp.exp(b_last + d_col - m_new) * k.astype(F32)
        c_ref[hh] = a_prev * c_prev + lax.dot_general(
            ka.astype(BF16), v, (((0,), (0,)), ((), ())), preferred_element_type=F32)
        n_ref[hh] = a_prev * n_prev + jnp.sum(ka, axis=0, keepdims=True)
        m_ref[hh] = jnp.broadcast_to(m_new, m_ref.shape[1:])


def mlstm_recurrence(proj, gates, b_i, b_f, head_g, batch, seq):
    heads = b_i.shape[0]
    dv = head_g.shape[0] // heads
    dqk = dv // 2
    chunk = _tile(seq, MLSTM_CHUNK)
    nc = seq // chunk
    hps = _tile(heads, MLSTM_HEADS_PER_STEP)
    proj3 = proj.reshape(batch, seq, proj.shape[1])
    gf = gates[:, :2 * heads].reshape(batch, nc, chunk, 2, heads).transpose(0, 4, 1, 3, 2)
    tri = (jnp.arange(chunk)[:, None] <= jnp.arange(chunk)[None, :]).astype(BF16)
    k_off = heads // hps
    v_off = 2 * heads * dqk // (dv * hps)
    o_off = v_off + heads // hps
    smem = pl.BlockSpec(memory_space=pltpu.SMEM)
    out = pl.pallas_call(
        functools.partial(_mlstm_kernel, chunk=chunk, dqk=dqk, dv=dv, heads_per_step=hps,
                          qk_scale=float(dqk) ** -0.5),
        grid=(batch, heads // hps, nc),
        in_specs=[
            smem, smem,
            pl.BlockSpec((None, chunk, hps * dqk), lambda b, h, c: (b, c, h)),
            pl.BlockSpec((None, chunk, hps * dqk), lambda b, h, c: (b, c, k_off + h)),
            pl.BlockSpec((None, chunk, hps * dv), lambda b, h, c: (b, c, v_off + h)),
            pl.BlockSpec((None, chunk, hps * dv), lambda b, h, c: (b, c, o_off + h)),
            pl.BlockSpec((None, hps, None, 2, chunk), lambda b, h, c: (b, h, c, 0, 0)),
            pl.BlockSpec((1, hps * dv), lambda b, h, c: (0, h)),
            pl.BlockSpec((chunk, chunk), lambda b, h, c: (0, 0)),
        ],
        out_specs=pl.BlockSpec((None, chunk, hps * dv), lambda b, h, c: (b, c, h)),
        out_shape=jax.ShapeDtypeStruct((batch, seq, heads * dv), BF16),
        scratch_shapes=[
            pltpu.VMEM((hps, dqk, dv), F32),
            pltpu.VMEM((hps, 1, dqk), F32),
            pltpu.VMEM((hps, SUBLANE, LANE), F32),
        ],
        compiler_params=_params(("parallel", "parallel", "arbitrary"), 48),
        name="mlstm_recurrence",
    )(b_i, b_f, proj3, proj3, proj3, proj3, gf, head_g.reshape(1, heads * dv), tri)
    return out.reshape(batch * seq, heads * dv)


def _ffn_in_kernel(a_ref, wg_ref, wu_ref, cwg_ref, cwu_ref, cbg_ref, cbu_ref, o_ref, hg_ref, hu_ref,
                   *, bm, tiles_per_seq):
    first = (pl.program_id(1) % tiles_per_seq) == 0

    @pl.when(first)
    def _():
        hg_ref[0:SUBLANE, :] = jnp.zeros((SUBLANE, hg_ref.shape[1]), F32)
        hu_ref[0:SUBLANE, :] = jnp.zeros((SUBLANE, hu_ref.shape[1]), F32)

    @pl.when(jnp.logical_not(first))
    def _():
        hg_ref[0:SUBLANE, :] = hg_ref[bm:bm + SUBLANE, :]
        hu_ref[0:SUBLANE, :] = hu_ref[bm:bm + SUBLANE, :]

    a = a_ref[...]

    def conv_half(w_ref, cw_ref, cb_ref, h_ref):
        hcur = jnp.dot(a, w_ref[...], preferred_element_type=F32)
        h_ref[SUBLANE:SUBLANE + bm, :] = hcur
        hall = h_ref[0:SUBLANE + bm, :]
        h1 = pltpu.roll(hall, 1, axis=0)[SUBLANE:, :]
        h2 = pltpu.roll(hall, 2, axis=0)[SUBLANE:, :]
        cw = cw_ref[...]
        return cb_ref[...] + cw[0:1, :] * h2 + cw[1:2, :] * h1 + cw[2:3, :] * hcur

    g = conv_half(wg_ref, cwg_ref, cbg_ref, hg_ref)
    u = conv_half(wu_ref, cwu_ref, cbu_ref, hu_ref)
    o_ref[...] = (jax.nn.silu(g) * u).astype(o_ref.dtype)


def ffn_in(a, w_in, conv_w, conv_b, seq, bm_pref=1024, bn_pref=512):
    t, d = a.shape
    fp = w_in.shape[1] // 2
    bm, bn = _tile(seq, bm_pref), _tile(fp, bn_pref)
    nj = fp // bn
    return pl.pallas_call(
        functools.partial(_ffn_in_kernel, bm=bm, tiles_per_seq=seq // bm),
        grid=(nj, t // bm),
        in_specs=[
            pl.BlockSpec((bm, d), lambda j, i: (i, 0)),
            pl.BlockSpec((d, bn), lambda j, i: (0, j)),
            pl.BlockSpec((d, bn), lambda j, i: (0, nj + j)),
            pl.BlockSpec((FFN_CONV_WIDTH, bn), lambda j, i: (0, j)),
            pl.BlockSpec((FFN_CONV_WIDTH, bn), lambda j, i: (0, nj + j)),
            pl.BlockSpec((1, bn), lambda j, i: (0, j)),
            pl.BlockSpec((1, bn), lambda j, i: (0, nj + j)),
        ],
        out_specs=pl.BlockSpec((bm, bn), lambda j, i: (i, j)),
        out_shape=jax.ShapeDtypeStruct((t, fp), BF16),
        scratch_shapes=[pltpu.VMEM((bm + 2 * SUBLANE, bn), F32), pltpu.VMEM((bm + 2 * SUBLANE, bn), F32)],
        compiler_params=_params(("parallel", "arbitrary"), 56),
        name="ffn_in",
    )(a, w_in, w_in, conv_w, conv_w, conv_b, conv_b)


def _pool_kernel(x_ref, g_ref, w_ref, sc_ref, gn_ref, o_ref, on_ref, e_ref, p_ref, *, bm, tiles_per_seq, group_dim):
    step = pl.program_id(0)
    tile_in_seq = step % tiles_per_seq
    halo = POOL_HALO
    rows = halo + bm

    @pl.when(step == 0)
    def _():
        e_ref[0:halo, :] = jnp.zeros((halo, e_ref.shape[1]), F32)
        p_ref[0:SUBLANE, :] = jnp.zeros((SUBLANE, p_ref.shape[1]), F32)

    @pl.when(jnp.logical_and(step > 0, tile_in_seq == 0))
    def _():
        e_ref[0:halo, :] = jnp.zeros((halo, e_ref.shape[1]), F32)

    @pl.when(tile_in_seq > 0)
    def _():
        e_ref[0:halo, :] = e_ref[bm:bm + halo, :]

    x = x_ref[...]
    xn = x * lax.rsqrt(jnp.mean(x * x, axis=-1, keepdims=True) + RMS_EPS) * g_ref[...]
    e_ref[halo:rows, :] = xn

    pos = tile_in_seq * bm + lax.broadcasted_iota(jnp.int32, (bm, 1), 0)
    lo = SUBLANE
    for gi, window in enumerate(POOL_WINDOWS):
        cs = slice(gi * group_dim, (gi + 1) * group_dim)
        src, dst, span = e_ref, p_ref, 1
        while span < window:
            dst[lo:rows, cs] = src[lo:rows, cs] + src[lo - span:rows - span, cs]
            src, dst = dst, (e_ref if dst is p_ref else p_ref)
            span *= 2
        wsum = src[halo:rows, cs]
        cnt = jnp.minimum(pos + 1, window).astype(F32)
        pooled = wsum / cnt - xn[:, cs]
        y = jnp.dot(pooled.astype(BF16), w_ref[gi], preferred_element_type=F32)
        o_ref[:, cs] = x[:, cs] + y * sc_ref[:, cs]
        if window > 2:
            e_ref[halo:rows, cs] = xn[:, cs]

    y = o_ref[...]
    on_ref[...] = (y * lax.rsqrt(jnp.mean(y * y, axis=-1, keepdims=True) + RMS_EPS) * gn_ref[...]).astype(on_ref.dtype)


def pool_mixer(x2d, norm_g, w_group, scale, next_norm_g, seq, bm_pref=256):
    t, d = x2d.shape
    groups, group_dim, _ = w_group.shape
    bm = _tile(seq, bm_pref)
    return pl.pallas_call(
        functools.partial(_pool_kernel, bm=bm, tiles_per_seq=seq // bm, group_dim=group_dim),
        grid=(t // bm,),
        in_specs=[
            pl.BlockSpec((bm, d), lambda i: (i, 0)),
            pl.BlockSpec((1, d), lambda i: (0, 0)),
            pl.BlockSpec((groups, group_dim, group_dim), lambda i: (0, 0, 0)),
            pl.BlockSpec((1, d), lambda i: (0, 0)),
            pl.BlockSpec((1, d), lambda i: (0, 0)),
        ],
        out_specs=[pl.BlockSpec((bm, d), lambda i: (i, 0)), pl.BlockSpec((bm, d), lambda i: (i, 0))],
        out_shape=[jax.ShapeDtypeStruct((t, d), F32), jax.ShapeDtypeStruct((t, d), BF16)],
        scratch_shapes=[pltpu.VMEM((POOL_HALO + bm, d), F32), pltpu.VMEM((POOL_HALO + bm, d), F32)],
        compiler_params=_params(("arbitrary",), 56),
        name="pool_mixer",
    )(x2d, norm_g.reshape(1, d), w_group, scale.reshape(1, d), next_norm_g.reshape(1, d))


def _prep_w_in_kernel(w_ref, o_ref, *, d_ff, fp):
    rows = w_ref.shape[0]
    full = (d_ff // LANE) * LANE
    tail = d_ff - full
    x = w_ref[...]
    o_ref[:, 0:full] = x[:, 0:full].astype(BF16)
    if tail == 0:
        up = x[:, d_ff:2 * d_ff]
    else:
        keep = lax.broadcasted_iota(jnp.int32, (rows, LANE), 1) < tail
        o_ref[:, full:full + LANE] = jnp.where(keep, x[:, full:full + LANE], 0.0).astype(BF16)
        src = x[:, full:2 * d_ff]
        up = pltpu.roll(src, src.shape[1] - tail, axis=1)
        o_ref[:, fp + full:fp + full + LANE] = jnp.where(keep, up[:, full:full + LANE], 0.0).astype(BF16)
    o_ref[:, fp:fp + full] = up[:, 0:full].astype(BF16)
    used = full + (LANE if tail else 0)
    if used < fp:
        zeros = jnp.zeros((rows, fp - used), BF16)
        o_ref[:, used:fp] = zeros
        o_ref[:, fp + used:2 * fp] = zeros


def prep_w_in(w_in_all, layer, fp):
    _, d, two_ff = w_in_all.shape
    d_ff = two_ff // 2
    rows = _tile(d, 128)
    return pl.pallas_call(
        functools.partial(_prep_w_in_kernel, d_ff=d_ff, fp=fp),
        grid=(d // rows,),
        in_specs=[pl.BlockSpec((None, rows, two_ff), lambda i: (layer, i, 0))],
        out_specs=pl.BlockSpec((rows, 2 * fp), lambda i: (i, 0)),
        out_shape=jax.ShapeDtypeStruct((d, 2 * fp), BF16),
        compiler_params=_params(("parallel",), 48),
        name="prep_w_in",
    )(w_in_all)


def _prep_w_out_kernel(w_ref, o_ref, *, n_src):
    @pl.when(pl.program_id(0) < n_src)
    def _():
        o_ref[...] = w_ref[...].astype(BF16)

    @pl.when(pl.program_id(0) >= n_src)
    def _():
        o_ref[...] = jnp.zeros_like(o_ref)


def prep_w_out(w_out_all, layer, fp):
    _, d_ff, d = w_out_all.shape
    unit = 64
    assert d_ff % unit == 0
    units = d_ff // unit
    rows = unit * max(c for c in range(1, units + 1)
                      if units % c == 0 and c * unit * d * 4 <= 10 * MIB)
    n_src = d_ff // rows
    n_blocks = -(-fp // rows)
    return pl.pallas_call(
        functools.partial(_prep_w_out_kernel, n_src=n_src),
        grid=(n_blocks,),
        in_specs=[pl.BlockSpec((None, rows, d), lambda i: (layer, jnp.minimum(i, n_src - 1), 0))],
        out_specs=pl.BlockSpec((rows, d), lambda i: (i, 0)),
        out_shape=jax.ShapeDtypeStruct((n_blocks * rows, d), BF16),
        compiler_params=_params(("parallel",), 40),
        name="prep_w_out",
    )(w_out_all)


def _pad_cols(w, width):
    return jnp.pad(w, [(0, 0)] * (w.ndim - 1) + [(0, width - w.shape[-1])])


FFN_OUT_K_STEPS = 4


def _padded_hidden(d_ff):
    unit = 1024 if d_ff > 1024 else LANE * FFN_OUT_K_STEPS
    return -(-d_ff // unit) * unit


def _conv_ffn(x2d, xn, w_in_all, conv_w, conv_b, w_out_all, layer, seq):
    d_ff = w_out_all.shape[1]
    fp = _padded_hidden(d_ff)
    w_in_p = prep_w_in(w_in_all, layer, fp)
    w_out_p = prep_w_out(w_out_all, layer, fp)
    conv_w_p = _pad_cols(conv_w.reshape(FFN_CONV_WIDTH, 2, d_ff), fp).reshape(FFN_CONV_WIDTH, 2 * fp)
    conv_b_p = _pad_cols(conv_b.reshape(1, 2, d_ff), fp).reshape(1, 2 * fp)
    act = ffn_in(xn, w_in_p, conv_w_p, conv_b_p, seq)
    return matmul_residual(act, w_out_p, x2d, FFN_OUT_K_STEPS)


def _mlstm_layer(x2d, norm_g, w_in_all, b_i, b_f, head_g, w_out_all, j, batch, seq):
    heads = b_i.shape[0]
    main = w_in_all.shape[2] - 2 * heads
    w_in = cast_layer(w_in_all, j)
    xn = rmsnorm(x2d, norm_g, BF16)
    proj = matmul(xn, w_in, BF16, n=main)
    gates = matmul(xn, _pad_cols(w_in[:, main:], LANE), F32)
    hgated = mlstm_recurrence(proj, gates, b_i, b_f, head_g, batch, seq)
    return matmul_residual(hgated, cast_layer(w_out_all, j), x2d, 1)


def kernel(x, mlstm_norm_g, mlstm_w_in, mlstm_b_i, mlstm_b_f, mlstm_head_g, mlstm_w_out,
           pool_norm_g, pool_w_group, pool_scale,
           ffn_norm_g, ffn_w_in, ffn_conv_w, ffn_conv_b, ffn_w_out, final_norm_g):
    batch, seq, d = x.shape
    depth = ffn_w_in.shape[0]
    x2d = x.reshape(batch * seq, d)
    for layer in range(depth):
        j = layer // 2
        if layer % 2 == 0:
            x2d = _mlstm_layer(x2d, mlstm_norm_g[j], mlstm_w_in, mlstm_b_i[j], mlstm_b_f[j],
                               mlstm_head_g[j], mlstm_w_out, j, batch, seq)
            xn = rmsnorm(x2d, ffn_norm_g[layer], BF16)
        else:
            x2d, xn = pool_mixer(x2d, pool_norm_g[j], pool_w_group[j].astype(BF16), pool_scale[j],
                                 ffn_norm_g[layer], seq)
        x2d = _conv_ffn(x2d, xn, ffn_w_in, ffn_conv_w[layer], ffn_conv_b[layer], ffn_w_out, layer, seq)
    return rmsnorm(x2d, final_norm_g, x.dtype).reshape(batch, seq, d)
```

```python
import functools

import jax
import jax.numpy as jnp
from jax import lax
from jax.experimental import pallas as pl
from jax.experimental.pallas import tpu as pltpu

RMS_EPS = 1e-6
GATE_SOFTCAP = 15.0
POOL_WINDOWS = (2, 4, 8, 16)
FFN_CONV_WIDTH = 3
MLSTM_CHUNK = 256
MLSTM_HEADS_PER_STEP = 2
LANE = 128
SUBLANE = 8
POOL_HALO = 32
MIB = 1024 * 1024

F32 = jnp.float32
BF16 = jnp.bfloat16


def _params(semantics, vmem_mib):
    return pltpu.CompilerParams(dimension_semantics=semantics, vmem_limit_bytes=vmem_mib * MIB)


def _tile(dim, pref):
    t = min(dim, pref)
    while dim % t:
        t //= 2
    return t


def _rmsnorm_kernel(x_ref, g_ref, o_ref):
    x = x_ref[...]
    y = x * lax.rsqrt(jnp.mean(x * x, axis=-1, keepdims=True) + RMS_EPS)
    o_ref[...] = (y * g_ref[...]).astype(o_ref.dtype)


def rmsnorm(x2d, g, out_dtype):
    t, d = x2d.shape
    bm = _tile(t, 256)
    return pl.pallas_call(
        _rmsnorm_kernel,
        grid=(t // bm,),
        in_specs=[pl.BlockSpec((bm, d), lambda i: (i, 0)), pl.BlockSpec((1, d), lambda i: (0, 0))],
        out_specs=pl.BlockSpec((bm, d), lambda i: (i, 0)),
        out_shape=jax.ShapeDtypeStruct((t, d), out_dtype),
        compiler_params=_params(("parallel",), 32),
        name="rmsnorm",
    )(x2d, g.reshape(1, d))


def _mm_kernel(a_ref, w_ref, *refs, side, n_inner):
    n_side_in = len(side.arrays) if side else 0
    o_ref = refs[n_side_in]
    o_ref[...] = jnp.dot(a_ref[...], w_ref[...], preferred_element_type=F32).astype(o_ref.dtype)
    if side:
        side.body(pl.program_id(0) * n_inner + pl.program_id(1), refs[:n_side_in], refs[n_side_in + 1:])


def matmul(a, w, out_dtype, n=None, side=None, bm_pref=1024, bn_pref=1024):
    m, k = a.shape
    n = w.shape[1] if n is None else n
    bm, bn = _tile(m, bm_pref), _tile(n, bn_pref)
    ni = m // bm
    if side and not side.plan((n // bn) * ni):
        side = None
    step = lambda j, i: j * ni + i
    side_in, side_out = side.specs(step) if side else ([], [])
    outs = pl.pallas_call(
        functools.partial(_mm_kernel, side=side, n_inner=ni),
        grid=(n // bn, ni),
        in_specs=[pl.BlockSpec((bm, k), lambda j, i: (i, 0)), pl.BlockSpec((k, bn), lambda j, i: (0, j))] + side_in,
        out_specs=[pl.BlockSpec((bm, bn), lambda j, i: (i, j))] + side_out,
        out_shape=[jax.ShapeDtypeStruct((m, n), out_dtype)] + (side.out_shapes if side else []),
        compiler_params=_params(("parallel", "arbitrary" if side else "parallel"), 60 if side else 56),
        name="matmul",
    )(a, w, *(side.arrays if side else ()))
    return outs[0], (tuple(outs[1:]) if side else None)


def _mm_res_kernel(a_ref, w_ref, x_ref, *refs, nk, side):
    n_side_in = len(side.arrays) if side else 0
    o_ref = refs[n_side_in]

    def partial_product():
        if side:
            step = (pl.program_id(0) * pl.num_programs(1) + pl.program_id(1)) * nk + pl.program_id(2)
            side.body(step, refs[:n_side_in], refs[n_side_in + 1:])
        return jnp.dot(a_ref[...], w_ref[...], preferred_element_type=F32)

    if nk == 1:
        o_ref[...] = x_ref[...] + partial_product()
        return
    kk = pl.program_id(2)

    @pl.when(kk == 0)
    def _():
        o_ref[...] = partial_product()

    @pl.when(jnp.logical_and(kk > 0, kk < nk - 1))
    def _():
        o_ref[...] += partial_product()

    @pl.when(kk == nk - 1)
    def _():
        o_ref[...] = x_ref[...] + (o_ref[...] + partial_product())


def matmul_residual(a, w, x, nk, side=None, bm_pref=1024, bn_pref=1024):
    m, k = a.shape
    n = w.shape[1]
    bm, bn, bk = _tile(m, bm_pref), _tile(n, bn_pref), k // nk
    assert bk * nk == k and (nk == 1 or bk % LANE == 0)
    nj = n // bn
    if side and not side.plan((m // bm) * nj * nk):
        side = None
    step = lambda i, j, kk: (i * nj + j) * nk + kk
    side_in, side_out = side.specs(step) if side else ([], [])
    outs = pl.pallas_call(
        functools.partial(_mm_res_kernel, nk=nk, side=side),
        grid=(m // bm, nj, nk),
        in_specs=[
            pl.BlockSpec((bm, bk), lambda i, j, kk: (i, kk)),
            pl.BlockSpec((bk, bn), lambda i, j, kk: (kk, j)),
            pl.BlockSpec((bm, bn), lambda i, j, kk: (i, j)),
        ] + side_in,
        out_specs=[pl.BlockSpec((bm, bn), lambda i, j, kk: (i, j))] + side_out,
        out_shape=[jax.ShapeDtypeStruct((m, n), F32)] + (side.out_shapes if side else []),
        compiler_params=_params(("arbitrary",) * 3 if side else ("parallel", "parallel", "arbitrary"), 56),
        name="matmul_residual",
    )(a, w, x, *(side.arrays if side else ()))
    return outs[0], (tuple(outs[1:]) if side else None)


def _as_column(row, length):
    return jnp.transpose(jnp.broadcast_to(row, (LANE, length)))[:, 0:1]


def _cumsum_lanes(row, tri):
    x = jnp.broadcast_to(row, (SUBLANE, row.shape[1]))
    hi = x.astype(BF16).astype(F32)
    r1 = x - hi
    mid = r1.astype(BF16).astype(F32)
    lo = (r1 - mid).astype(BF16).astype(F32)
    parts = jnp.concatenate([hi, mid, lo], axis=0).astype(BF16)
    acc = jnp.dot(parts, tri, preferred_element_type=F32)
    s = acc[0:SUBLANE] + acc[SUBLANE:2 * SUBLANE] + acc[2 * SUBLANE:3 * SUBLANE]
    return s[0:1, :]


def _mlstm_kernel(bi_ref, bf_ref, q_ref, k_ref, v_ref, o_ref, gf_ref, hg_ref, tri_ref, out_ref,
                  c_ref, n_ref, m_ref, *, chunk, dqk, dv, heads_per_step, qk_scale):
    @pl.when(pl.program_id(2) == 0)
    def _():
        c_ref[...] = jnp.zeros_like(c_ref)
        n_ref[...] = jnp.zeros_like(n_ref)
        m_ref[...] = jnp.zeros_like(m_ref)

    tri = tri_ref[...]
    t_idx = lax.broadcasted_iota(jnp.int32, (chunk, chunk), 0)
    s_idx = lax.broadcasted_iota(jnp.int32, (chunk, chunk), 1)
    causal = t_idx >= s_idx

    for hh in range(heads_per_step):
        h = pl.program_id(1) * heads_per_step + hh
        qs = slice(hh * dqk, (hh + 1) * dqk)
        vs = slice(hh * dv, (hh + 1) * dv)
        gf = gf_ref[hh]
        i_row = GATE_SOFTCAP * jnp.tanh((gf[0:1, :] + bi_ref[h]) / GATE_SOFTCAP)
        logf = jax.nn.log_sigmoid(GATE_SOFTCAP * jnp.tanh((gf[1:2, :] + bf_ref[h]) / GATE_SOFTCAP))
        b_row = _cumsum_lanes(logf, tri)
        d_row = i_row - b_row
        b_col = _as_column(b_row, chunk)
        d_col = _as_column(d_row, chunk)

        log_intra = jnp.where(causal, b_col + d_row, -jnp.inf)
        m_prev = m_ref[hh, 0:1, 0:1]
        log_inter = b_col + m_prev
        m_t = jnp.maximum(log_inter, jnp.max(log_intra, axis=1, keepdims=True))
        w_inter = jnp.exp(log_inter - m_t)
        w_intra = jnp.exp(log_intra - m_t)

        q = q_ref[:, qs] * qk_scale
        k = k_ref[:, qs]
        v = v_ref[:, vs]
        s = lax.dot_general(q, k, (((1,), (1,)), ((), ())), preferred_element_type=F32) * w_intra
        c_prev = c_ref[hh]
        n_prev = n_ref[hh]
        num = (w_inter * jnp.dot(q, c_prev.astype(BF16), preferred_element_type=F32)
               + jnp.dot(s.astype(BF16), v, preferred_element_type=F32))
        qn = jnp.sum(q.astype(F32) * n_prev, axis=1, keepdims=True)
        den = w_inter * qn + jnp.sum(s, axis=1, keepdims=True)
        hval = num / jnp.maximum(jnp.abs(den), jnp.exp(-m_t))
        hval = hval * lax.rsqrt(jnp.mean(hval * hval, axis=-1, keepdims=True) + RMS_EPS)
        hval = hval * hg_ref[:, vs]
        out_ref[:, vs] = (hval * jax.nn.sigmoid(o_ref[:, vs].astype(F32))).astype(out_ref.dtype)

        b_last = b_row[:, chunk - 1:chunk]
        log_prev = b_last + m_prev
        m_new = jnp.maximum(log_prev, jnp.max(b_last + d_row, axis=1, keepdims=True))
        a_prev = jnp.exp(log_prev - m_new)
        ka = jnp.exp(b_last + d_col - m_new) * k.astype(F32)
        c_ref[hh] = a_prev * c_prev + lax.dot_general(
            ka.astype(BF16), v, (((0,), (0,)), ((), ())), preferred_element_type=F32)
        n_ref[hh] = a_prev * n_prev + jnp.sum(ka, axis=0, keepdims=True)
        m_ref[hh] = jnp.broadcast_to(m_new, m_ref.shape[1:])


def mlstm_recurrence(proj, gates, b_i, b_f, head_g, batch, seq):
    heads = b_i.shape[0]
    dv = head_g.shape[0] // heads
    dqk = dv // 2
    chunk = _tile(seq, MLSTM_CHUNK)
    nc = seq // chunk
    hps = _tile(heads, MLSTM_HEADS_PER_STEP)
    proj3 = proj.reshape(batch, seq, proj.shape[1])
    gf = gates[:, :2 * heads].reshape(batch, nc, chunk, 2, heads).transpose(0, 4, 1, 3, 2)
    tri = (jnp.arange(chunk)[:, None] <= jnp.arange(chunk)[None, :]).astype(BF16)
    k_off = heads // hps
    v_off = 2 * heads * dqk // (dv * hps)
    o_off = v_off + heads // hps
    smem = pl.BlockSpec(memory_space=pltpu.SMEM)
    out = pl.pallas_call(
        functools.partial(_mlstm_kernel, chunk=chunk, dqk=dqk, dv=dv, heads_per_step=hps,
                          qk_scale=float(dqk) ** -0.5),
        grid=(batch, heads // hps, nc),
        in_specs=[
            smem, smem,
            pl.BlockSpec((None, chunk, hps * dqk), lambda b, h, c: (b, c, h)),
            pl.BlockSpec((None, chunk, hps * dqk), lambda b, h, c: (b, c, k_off + h)),
            pl.BlockSpec((None, chunk, hps * dv), lambda b, h, c: (b, c, v_off + h)),
            pl.BlockSpec((None, chunk, hps * dv), lambda b, h, c: (b, c, o_off + h)),
            pl.BlockSpec((None, hps, None, 2, chunk), lambda b, h, c: (b, h, c, 0, 0)),
            pl.BlockSpec((1, hps * dv), lambda b, h, c: (0, h)),
            pl.BlockSpec((chunk, chunk), lambda b, h, c: (0, 0)),
        ],
        out_specs=pl.BlockSpec((None, chunk, hps * dv), lambda b, h, c: (b, c, h)),
        out_shape=jax.ShapeDtypeStruct((batch, seq, heads * dv), BF16),
        scratch_shapes=[
            pltpu.VMEM((hps, dqk, dv), F32),
            pltpu.VMEM((hps, 1, dqk), F32),
            pltpu.VMEM((hps, SUBLANE, LANE), F32),
        ],
        compiler_params=_params(("parallel", "parallel", "arbitrary"), 48),
        name="mlstm_recurrence",
    )(b_i, b_f, proj3, proj3, proj3, proj3, gf, head_g.reshape(1, heads * dv), tri)
    return out.reshape(batch * seq, heads * dv)


def _ffn_in_kernel(a_ref, wg_ref, wu_ref, cwg_ref, cwu_ref, cbg_ref, cbu_ref, o_ref, hg_ref, hu_ref,
                   *, bm, tiles_per_seq):
    first = (pl.program_id(1) % tiles_per_seq) == 0

    @pl.when(first)
    def _():
        hg_ref[0:SUBLANE, :] = jnp.zeros((SUBLANE, hg_ref.shape[1]), F32)
        hu_ref[0:SUBLANE, :] = jnp.zeros((SUBLANE, hu_ref.shape[1]), F32)

    @pl.when(jnp.logical_not(first))
    def _():
        hg_ref[0:SUBLANE, :] = hg_ref[bm:bm + SUBLANE, :]
        hu_ref[0:SUBLANE, :] = hu_ref[bm:bm + SUBLANE, :]

    a = a_ref[...]

    def conv_half(w_ref, cw_ref, cb_ref, h_ref):
        hcur = jnp.dot(a, w_ref[...], preferred_element_type=F32)
        h_ref[SUBLANE:SUBLANE + bm, :] = hcur
        hall = h_ref[0:SUBLANE + bm, :]
        h1 = pltpu.roll(hall, 1, axis=0)[SUBLANE:, :]
        h2 = pltpu.roll(hall, 2, axis=0)[SUBLANE:, :]
        cw = cw_ref[...]
        return cb_ref[...] + cw[0:1, :] * h2 + cw[1:2, :] * h1 + cw[2:3, :] * hcur

    g = conv_half(wg_ref, cwg_ref, cbg_ref, hg_ref)
    u = conv_half(wu_ref, cwu_ref, cbu_ref, hu_ref)
    o_ref[...] = (jax.nn.silu(g) * u).astype(o_ref.dtype)


def ffn_in(a, w_in, conv_w, conv_b, seq, bm_pref=1024, bn_pref=512):
    t, d = a.shape
    fp = w_in.shape[1] // 2
    bm, bn = _tile(seq, bm_pref), _tile(fp, bn_pref)
    nj = fp // bn
    return pl.pallas_call(
        functools.partial(_ffn_in_kernel, bm=bm, tiles_per_seq=seq // bm),
        grid=(nj, t // bm),
        in_specs=[
            pl.BlockSpec((bm, d), lambda j, i: (i, 0)),
            pl.BlockSpec((d, bn), lambda j, i: (0, j)),
            pl.BlockSpec((d, bn), lambda j, i: (0, nj + j)),
            pl.BlockSpec((FFN_CONV_WIDTH, bn), lambda j, i: (0, j)),
            pl.BlockSpec((FFN_CONV_WIDTH, bn), lambda j, i: (0, nj + j)),
            pl.BlockSpec((1, bn), lambda j, i: (0, j)),
            pl.BlockSpec((1, bn), lambda j, i: (0, nj + j)),
        ],
        out_specs=pl.BlockSpec((bm, bn), lambda j, i: (i, j)),
        out_shape=jax.ShapeDtypeStruct((t, fp), BF16),
        scratch_shapes=[pltpu.VMEM((bm + 2 * SUBLANE, bn), F32), pltpu.VMEM((bm + 2 * SUBLANE, bn), F32)],
        compiler_params=_params(("parallel", "arbitrary"), 56),
        name="ffn_in",
    )(a, w_in, w_in, conv_w, conv_w, conv_b, conv_b)


def _pool_kernel(x_ref, g_ref, w_ref, sc_ref, gn_ref, o_ref, on_ref, e_ref, p_ref, *, bm, tiles_per_seq, group_dim):
    step = pl.program_id(0)
    tile_in_seq = step % tiles_per_seq
    halo = POOL_HALO
    rows = halo + bm

    @pl.when(step == 0)
    def _():
        e_ref[0:halo, :] = jnp.zeros((halo, e_ref.shape[1]), F32)
        p_ref[0:SUBLANE, :] = jnp.zeros((SUBLANE, p_ref.shape[1]), F32)

    @pl.when(jnp.logical_and(step > 0, tile_in_seq == 0))
    def _():
        e_ref[0:halo, :] = jnp.zeros((halo, e_ref.shape[1]), F32)

    @pl.when(tile_in_seq > 0)
    def _():
        e_ref[0:halo, :] = e_ref[bm:bm + halo, :]

    x = x_ref[...]
    xn = x * lax.rsqrt(jnp.mean(x * x, axis=-1, keepdims=True) + RMS_EPS) * g_ref[...]
    e_ref[halo:rows, :] = xn

    pos = tile_in_seq * bm + lax.broadcasted_iota(jnp.int32, (bm, 1), 0)
    lo = SUBLANE
    for gi, window in enumerate(POOL_WINDOWS):
        cs = slice(gi * group_dim, (gi + 1) * group_dim)
        src, dst, span = e_ref, p_ref, 1
        while span < window:
            dst[lo:rows, cs] = src[lo:rows, cs] + src[lo - span:rows - span, cs]
            src, dst = dst, (e_ref if dst is p_ref else p_ref)
            span *= 2
        wsum = src[halo:rows, cs]
        cnt = jnp.minimum(pos + 1, window).astype(F32)
        pooled = wsum / cnt - xn[:, cs]
        y = jnp.dot(pooled.astype(BF16), w_ref[gi], preferred_element_type=F32)
        o_ref[:, cs] = x[:, cs] + y * sc_ref[:, cs]
        if window > 2:
            e_ref[halo:rows, cs] = xn[:, cs]

    y = o_ref[...]
    on_ref[...] = (y * lax.rsqrt(jnp.mean(y * y, axis=-1, keepdims=True) + RMS_EPS) * gn_ref[...]).astype(on_ref.dtype)


def pool_mixer(x2d, norm_g, w_group, scale, next_norm_g, seq, bm_pref=256):
    t, d = x2d.shape
    groups, group_dim, _ = w_group.shape
    bm = _tile(seq, bm_pref)
    return pl.pallas_call(
        functools.partial(_pool_kernel, bm=bm, tiles_per_seq=seq // bm, group_dim=group_dim),
        grid=(t // bm,),
        in_specs=[
            pl.BlockSpec((bm, d), lambda i: (i, 0)),
            pl.BlockSpec((1, d), lambda i: (0, 0)),
            pl.BlockSpec((groups, group_dim, group_dim), lambda i: (0, 0, 0)),
            pl.BlockSpec((1, d), lambda i: (0, 0)),
            pl.BlockSpec((1, d), lambda i: (0, 0)),
        ],
        out_specs=[pl.BlockSpec((bm, d), lambda i: (i, 0)), pl.BlockSpec((bm, d), lambda i: (i, 0))],
        out_shape=[jax.ShapeDtypeStruct((t, d), F32), jax.ShapeDtypeStruct((t, d), BF16)],
        scratch_shapes=[pltpu.VMEM((POOL_HALO + bm, d), F32), pltpu.VMEM((POOL_HALO + bm, d), F32)],
        compiler_params=_params(("arbitrary",), 56),
        name="pool_mixer",
    )(x2d, norm_g.reshape(1, d), w_group, scale.reshape(1, d), next_norm_g.reshape(1, d))


def _prep_w_in_kernel(w_ref, o_ref, *, d_ff, fp):
    rows = w_ref.shape[0]
    full = (d_ff // LANE) * LANE
    tail = d_ff - full
    x = w_ref[...]
    o_ref[:, 0:full] = x[:, 0:full].astype(BF16)
    if tail == 0:
        up = x[:, d_ff:2 * d_ff]
    else:
        keep = lax.broadcasted_iota(jnp.int32, (rows, LANE), 1) < tail
        o_ref[:, full:full + LANE] = jnp.where(keep, x[:, full:full + LANE], 0.0).astype(BF16)
        src = x[:, full:2 * d_ff]
        up = pltpu.roll(src, src.shape[1] - tail, axis=1)
        o_ref[:, fp + full:fp + full + LANE] = jnp.where(keep, up[:, full:full + LANE], 0.0).astype(BF16)
    o_ref[:, fp:fp + full] = up[:, 0:full].astype(BF16)
    used = full + (LANE if tail else 0)
    if used < fp:
        zeros = jnp.zeros((rows, fp - used), BF16)
        o_ref[:, used:fp] = zeros
        o_ref[:, fp + used:2 * fp] = zeros


def prep_w_in(w_in_all, layer, fp):
    _, d, two_ff = w_in_all.shape
    d_ff = two_ff // 2
    rows = _tile(d, 128)
    return pl.pallas_call(
        functools.partial(_prep_w_in_kernel, d_ff=d_ff, fp=fp),
        grid=(d // rows,),
        in_specs=[pl.BlockSpec((None, rows, two_ff), lambda i: (layer, i, 0))],
        out_specs=pl.BlockSpec((rows, 2 * fp), lambda i: (i, 0)),
        out_shape=jax.ShapeDtypeStruct((d, 2 * fp), BF16),
        compiler_params=_params(("parallel",), 48),
        name="prep_w_in",
    )(w_in_all)


def _prep_w_out_kernel(w_ref, o_ref, *, n_src):
    @pl.when(pl.program_id(0) < n_src)
    def _():
        o_ref[...] = w_ref[...].astype(BF16)

    @pl.when(pl.program_id(0) >= n_src)
    def _():
        o_ref[...] = jnp.zeros_like(o_ref)


def prep_w_out(w_out_all, layer, fp):
    _, d_ff, d = w_out_all.shape
    unit = 64
    assert d_ff % unit == 0
    units = d_ff // unit
    rows = unit * max(c for c in range(1, units + 1)
                      if units % c == 0 and c * unit * d * 4 <= 10 * MIB)
    n_src = d_ff // rows
    n_blocks = -(-fp // rows)
    return pl.pallas_call(
        functools.partial(_prep_w_out_kernel, n_src=n_src),
        grid=(n_blocks,),
        in_specs=[pl.BlockSpec((None, rows, d), lambda i: (layer, jnp.minimum(i, n_src - 1), 0))],
        out_specs=pl.BlockSpec((rows, d), lambda i: (i, 0)),
        out_shape=jax.ShapeDtypeStruct((n_blocks * rows, d), BF16),
        compiler_params=_params(("parallel",), 40),
        name="prep_w_out",
    )(w_out_all)


class FfnWeightPrep:
    MAX_BLOCK_BYTES = 4 * MIB

    def __init__(self, w_in_all, w_out_all, layer, fp):
        _, d, two_ff = w_in_all.shape
        self.d, self.d_ff, self.fp, self.layer = d, two_ff // 2, fp, layer
        self.arrays = (w_in_all, w_out_all)
        self.out_shapes = [jax.ShapeDtypeStruct((d, 2 * fp), BF16), jax.ShapeDtypeStruct((fp, d), BF16)]

    @staticmethod
    def _rows(total, unit, max_blocks):
        for c in range(1, total // unit + 1):
            if total % (c * unit) == 0 and total // (c * unit) <= max_blocks:
                return c * unit
        return None

    def plan(self, n_steps):
        unit_out = 16
        while self.d_ff % (unit_out * 2) == 0 and self.fp % (unit_out * 2) == 0:
            unit_out *= 2
        self.in_rows = self._rows(self.d, 16, n_steps)
        self.out_rows = self._rows(self.fp, unit_out, n_steps)
        if self.in_rows is None or self.out_rows is None or self.d_ff % self.out_rows:
            return False
        self.n_in, self.n_src, self.n_out = self.d // self.in_rows, self.d_ff // self.out_rows, self.fp // self.out_rows
        return max(self.in_rows * 2 * self.d_ff, self.out_rows * self.d) * 4 <= self.MAX_BLOCK_BYTES

    def specs(self, step):
        layer, n_in, n_src, n_out = self.layer, self.n_in, self.n_src, self.n_out
        ins = [pl.BlockSpec((None, self.in_rows, 2 * self.d_ff),
                            lambda *g: (layer, jnp.minimum(step(*g), n_in - 1), 0)),
               pl.BlockSpec((None, self.out_rows, self.d),
                            lambda *g: (layer, jnp.minimum(step(*g), n_src - 1), 0))]
        outs = [pl.BlockSpec((self.in_rows, 2 * self.fp), lambda *g: (jnp.minimum(step(*g), n_in - 1), 0)),
                pl.BlockSpec((self.out_rows, self.d), lambda *g: (jnp.minimum(step(*g), n_out - 1), 0))]
        return ins, outs

    def body(self, step, in_refs, out_refs):
        _prep_w_in_kernel(in_refs[0], out_refs[0], d_ff=self.d_ff, fp=self.fp)
        real = jnp.minimum(step, self.n_out - 1) < self.n_src
        out_refs[1][...] = jnp.where(real, in_refs[1][...], 0.0).astype(BF16)


def _pad_cols(w, width):
    return jnp.pad(w, [(0, 0)] * (w.ndim - 1) + [(0, width - w.shape[-1])])


FFN_OUT_K_STEPS = 4


def _padded_hidden(d_ff):
    unit = 1024 if d_ff > 1024 else LANE * FFN_OUT_K_STEPS
    return -(-d_ff // unit) * unit


def _conv_ffn(x2d, xn, w_in_all, conv_w, conv_b, w_out_all, layer, seq, prepared=None, side=None):
    d_ff = w_out_all.shape[1]
    fp = _padded_hidden(d_ff)
    w_in_p, w_out_p = prepared or (prep_w_in(w_in_all, layer, fp), prep_w_out(w_out_all, layer, fp))
    conv_w_p = _pad_cols(conv_w.reshape(FFN_CONV_WIDTH, 2, d_ff), fp).reshape(FFN_CONV_WIDTH, 2 * fp)
    conv_b_p = _pad_cols(conv_b.reshape(1, 2, d_ff), fp).reshape(1, 2 * fp)
    act = ffn_in(xn, w_in_p, conv_w_p, conv_b_p, seq)
    return matmul_residual(act, w_out_p, x2d, FFN_OUT_K_STEPS, side=side)


def _mlstm_layer(x2d, norm_g, w_in, b_i, b_f, head_g, w_out, batch, seq, side=None):
    heads = b_i.shape[0]
    main = w_in.shape[1] - 2 * heads
    xn = rmsnorm(x2d, norm_g, BF16)
    proj, side_out = matmul(xn, w_in.astype(BF16), BF16, n=main, side=side)
    gates, _ = matmul(xn, _pad_cols(w_in[:, main:], LANE).astype(BF16), F32)
    hgated = mlstm_recurrence(proj, gates, b_i, b_f, head_g, batch, seq)
    out, _ = matmul_residual(hgated, w_out.astype(BF16), x2d, 1)
    return out, side_out


def kernel(x, mlstm_norm_g, mlstm_w_in, mlstm_b_i, mlstm_b_f, mlstm_head_g, mlstm_w_out,
           pool_norm_g, pool_w_group, pool_scale,
           ffn_norm_g, ffn_w_in, ffn_conv_w, ffn_conv_b, ffn_w_out, final_norm_g):
    batch, seq, d = x.shape
    depth = ffn_w_in.shape[0]
    fp = _padded_hidden(ffn_w_out.shape[1])
    x2d = x.reshape(batch * seq, d)
    prepared = {}
    for layer in range(depth):
        j = layer // 2
        if layer % 2 == 0:
            x2d, prepared[layer] = _mlstm_layer(x2d, mlstm_norm_g[j], mlstm_w_in[j], mlstm_b_i[j], mlstm_b_f[j],
                                                mlstm_head_g[j], mlstm_w_out[j], batch, seq,
                                                side=FfnWeightPrep(ffn_w_in, ffn_w_out, layer, fp))
            xn = rmsnorm(x2d, ffn_norm_g[layer], BF16)
            side = FfnWeightPrep(ffn_w_in, ffn_w_out, layer + 1, fp) if layer + 1 < depth else None
        else:
            x2d, xn = pool_mixer(x2d, pool_norm_g[j], pool_w_group[j].astype(BF16), pool_scale[j],
                                 ffn_norm_g[layer], seq)
            side = None
        x2d, prepared[layer + 1] = _conv_ffn(x2d, xn, ffn_w_in, ffn_conv_w[layer], ffn_conv_b[layer], ffn_w_out,
                                             layer, seq, prepared=prepared.get(layer), side=side)
    return rmsnorm(x2d, final_norm_g, x.dtype).reshape(batch, seq, d)
```

```python
import functools

import jax
import jax.numpy as jnp
from jax import lax
from jax.experimental import pallas as pl
from jax.experimental.pallas import tpu as pltpu

RMS_EPS = 1e-6
GATE_SOFTCAP = 15.0
POOL_WINDOWS = (2, 4, 8, 16)
FFN_CONV_WIDTH = 3
MLSTM_CHUNK = 256
MLSTM_HEADS_PER_STEP = 4
LANE = 128
SUBLANE = 8
POOL_HALO = 32
MIB = 1024 * 1024

F32 = jnp.float32
BF16 = jnp.bfloat16


def _params(semantics, vmem_mib):
    return pltpu.CompilerParams(dimension_semantics=semantics, vmem_limit_bytes=vmem_mib * MIB)


def _tile(dim, pref):
    t = min(dim, pref)
    while dim % t:
        t //= 2
    return t


def _rmsnorm_kernel(x_ref, g_ref, o_ref):
    x = x_ref[...]
    y = x * lax.rsqrt(jnp.mean(x * x, axis=-1, keepdims=True) + RMS_EPS)
    o_ref[...] = (y * g_ref[...]).astype(o_ref.dtype)


def rmsnorm(x2d, g, out_dtype):
    t, d = x2d.shape
    bm = _tile(t, 512)
    return pl.pallas_call(
        _rmsnorm_kernel,
        grid=(t // bm,),
        in_specs=[pl.BlockSpec((bm, d), lambda i: (i, 0)), pl.BlockSpec((1, d), lambda i: (0, 0))],
        out_specs=pl.BlockSpec((bm, d), lambda i: (i, 0)),
        out_shape=jax.ShapeDtypeStruct((t, d), out_dtype),
        compiler_params=_params(("parallel",), 48),
        name="rmsnorm",
    )(x2d, g.reshape(1, d))


def _mm_kernel(a_ref, w_ref, *refs, side, n_inner):
    n_side_in = len(side.arrays) if side else 0
    o_ref = refs[n_side_in]
    o_ref[...] = jnp.dot(a_ref[...], w_ref[...], preferred_element_type=F32).astype(o_ref.dtype)
    if side:
        side.body(pl.program_id(0) * n_inner + pl.program_id(1), refs[:n_side_in], refs[n_side_in + 1:])


def matmul(a, w, out_dtype, n=None, side=None, bm_pref=1024, bn_pref=1024):
    m, k = a.shape
    n = w.shape[1] if n is None else n
    bm, bn = _tile(m, bm_pref), _tile(n, bn_pref)
    ni = m // bm
    if side and not side.plan((n // bn) * ni):
        side = None
    step = lambda j, i: j * ni + i
    side_in, side_out = side.specs(step) if side else ([], [])
    outs = pl.pallas_call(
        functools.partial(_mm_kernel, side=side, n_inner=ni),
        grid=(n // bn, ni),
        in_specs=[pl.BlockSpec((bm, k), lambda j, i: (i, 0)), pl.BlockSpec((k, bn), lambda j, i: (0, j))] + side_in,
        out_specs=[pl.BlockSpec((bm, bn), lambda j, i: (i, j))] + side_out,
        out_shape=[jax.ShapeDtypeStruct((m, n), out_dtype)] + (side.out_shapes if side else []),
        compiler_params=_params(("parallel", "arbitrary" if side else "parallel"), 60 if side else 56),
        name="matmul",
    )(a, w, *(side.arrays if side else ()))
    return outs[0], (tuple(outs[1:]) if side else None)


def _mm_res_kernel(a_ref, w_ref, x_ref, *refs, nk, side):
    n_side_in = len(side.arrays) if side else 0
    o_ref = refs[n_side_in]

    def partial_product():
        if side:
            step = (pl.program_id(0) * pl.num_programs(1) + pl.program_id(1)) * nk + pl.program_id(2)
            side.body(step, refs[:n_side_in], refs[n_side_in + 1:])
        return jnp.dot(a_ref[...], w_ref[...], preferred_element_type=F32)

    if nk == 1:
        o_ref[...] = x_ref[...] + partial_product()
        return
    kk = pl.program_id(2)

    @pl.when(kk == 0)
    def _():
        o_ref[...] = partial_product()

    @pl.when(jnp.logical_and(kk > 0, kk < nk - 1))
    def _():
        o_ref[...] += partial_product()

    @pl.when(kk == nk - 1)
    def _():
        o_ref[...] = x_ref[...] + (o_ref[...] + partial_product())


def matmul_residual(a, w, x, nk, side=None, bm_pref=1024, bn_pref=1024):
    m, k = a.shape
    n = w.shape[1]
    bm, bn, bk = _tile(m, bm_pref), _tile(n, bn_pref), k // nk
    assert bk * nk == k and (nk == 1 or bk % LANE == 0)
    nj = n // bn
    if side and not side.plan((m // bm) * nj * nk):
        side = None
    step = lambda i, j, kk: (i * nj + j) * nk + kk
    side_in, side_out = side.specs(step) if side else ([], [])
    outs = pl.pallas_call(
        functools.partial(_mm_res_kernel, nk=nk, side=side),
        grid=(m // bm, nj, nk),
        in_specs=[
            pl.BlockSpec((bm, bk), lambda i, j, kk: (i, kk)),
            pl.BlockSpec((bk, bn), lambda i, j, kk: (kk, j)),
            pl.BlockSpec((bm, bn), lambda i, j, kk: (i, j)),
        ] + side_in,
        out_specs=[pl.BlockSpec((bm, bn), lambda i, j, kk: (i, j))] + side_out,
        out_shape=[jax.ShapeDtypeStruct((m, n), F32)] + (side.out_shapes if side else []),
        compiler_params=_params(("arbitrary",) * 3 if side else ("parallel", "parallel", "arbitrary"), 56),
        name="matmul_residual",
    )(a, w, x, *(side.arrays if side else ()))
    return outs[0], (tuple(outs[1:]) if side else None)


def _as_column(row, length):
    return jnp.transpose(jnp.broadcast_to(row, (LANE, length)))[:, 0:1]


def _cumsum_lanes(row, tri):
    x = jnp.broadcast_to(row, (SUBLANE, row.shape[1]))
    hi = x.astype(BF16).astype(F32)
    r1 = x - hi
    mid = r1.astype(BF16).astype(F32)
    lo = (r1 - mid).astype(BF16).astype(F32)
    parts = jnp.concatenate([hi, mid, lo], axis=0).astype(BF16)
    acc = jnp.dot(parts, tri, preferred_element_type=F32)
    s = acc[0:SUBLANE] + acc[SUBLANE:2 * SUBLANE] + acc[2 * SUBLANE:3 * SUBLANE]
    return s[0:1, :]


def _mlstm_kernel(bi_ref, bf_ref, q_ref, k_ref, v_ref, o_ref, gf_ref, hg_ref, tri_ref, out_ref,
                  c_ref, n_ref, m_ref, *, chunk, dqk, dv, heads_per_step, qk_scale):
    @pl.when(pl.program_id(2) == 0)
    def _():
        c_ref[...] = jnp.zeros_like(c_ref)
        n_ref[...] = jnp.zeros_like(n_ref)
        m_ref[...] = jnp.zeros_like(m_ref)

    tri = tri_ref[...]
    t_idx = lax.broadcasted_iota(jnp.int32, (chunk, chunk), 0)
    s_idx = lax.broadcasted_iota(jnp.int32, (chunk, chunk), 1)
    causal = t_idx >= s_idx

    for hh in range(heads_per_step):
        h = pl.program_id(1) * heads_per_step + hh
        qs = slice(hh * dqk, (hh + 1) * dqk)
        vs = slice(hh * dv, (hh + 1) * dv)
        gf = gf_ref[hh]
        i_row = GATE_SOFTCAP * jnp.tanh((gf[0:1, :] + bi_ref[h]) / GATE_SOFTCAP)
        logf = jax.nn.log_sigmoid(GATE_SOFTCAP * jnp.tanh((gf[1:2, :] + bf_ref[h]) / GATE_SOFTCAP))
        b_row = _cumsum_lanes(logf, tri)
        d_row = i_row - b_row
        b_col = _as_column(b_row, chunk)
        d_col = _as_column(d_row, chunk)

        log_intra = jnp.where(causal, b_col + d_row, -jnp.inf)
        m_prev = m_ref[hh, 0:1, 0:1]
        log_inter = b_col + m_prev
        m_t = jnp.maximum(log_inter, jnp.max(log_intra, axis=1, keepdims=True))
        w_inter = jnp.exp(log_inter - m_t)
        w_intra = jnp.exp(log_intra - m_t)

        q = q_ref[:, qs] * qk_scale
        k = k_ref[:, qs]
        v = v_ref[:, vs]
        s = lax.dot_general(q, k, (((1,), (1,)), ((), ())), preferred_element_type=F32) * w_intra
        c_prev = c_ref[hh]
        n_prev = n_ref[hh]
        num = (w_inter * jnp.dot(q, c_prev.astype(BF16), preferred_element_type=F32)
               + jnp.dot(s.astype(BF16), v, preferred_element_type=F32))
        qn = jnp.sum(q.astype(F32) * n_prev, axis=1, keepdims=True)
        den = w_inter * qn + jnp.sum(s, axis=1, keepdims=True)
        hval = num / jnp.maximum(jnp.abs(den), jnp.exp(-m_t))
        hval = hval * lax.rsqrt(jnp.mean(hval * hval, axis=-1, keepdims=True) + RMS_EPS)
        hval = hval * hg_ref[:, vs]
        out_ref[:, vs] = (hval * jax.nn.sigmoid(o_ref[:, vs].astype(F32))).astype(out_ref.dtype)

        b_last = b_row[:, chunk - 1:chunk]
        log_prev = b_last + m_prev
        m_new = jnp.maximum(log_prev, jnp.max(b_last + d_row, axis=1, keepdims=True))
        a_prev = jnp.exp(log_prev - m_new)
        ka = jnp.exp(b_last + d_col - m_new) * k.astype(F32)
        c_ref[hh] = a_prev * c_prev + lax.dot_general(
            ka.astype(BF16), v, (((0,), (0,)), ((), ())), preferred_element_type=F32)
        n_ref[hh] = a_prev * n_prev + jnp.sum(ka, axis=0, keepdims=True)
        m_ref[hh] = jnp.broadcast_to(m_new, m_ref.shape[1:])


def mlstm_recurrence(proj, gates, b_i, b_f, head_g, batch, seq):
    heads = b_i.shape[0]
    dv = head_g.shape[0] // heads
    dqk = dv // 2
    chunk = _tile(seq, MLSTM_CHUNK)
    nc = seq // chunk
    hps = _tile(heads, MLSTM_HEADS_PER_STEP)
    proj3 = proj.reshape(batch, seq, proj.shape[1])
    gf = gates[:, :2 * heads].reshape(batch, nc, chunk, 2, heads).transpose(0, 4, 1, 3, 2)
    tri = (jnp.arange(chunk)[:, None] <= jnp.arange(chunk)[None, :]).astype(BF16)
    k_off = heads // hps
    v_off = 2 * heads * dqk // (dv * hps)
    o_off = v_off + heads // hps
    smem = pl.BlockSpec(memory_space=pltpu.SMEM)
    out = pl.pallas_call(
        functools.partial(_mlstm_kernel, chunk=chunk, dqk=dqk, dv=dv, heads_per_step=hps,
                          qk_scale=float(dqk) ** -0.5),
        grid=(batch, heads // hps, nc),
        in_specs=[
            smem, smem,
            pl.BlockSpec((None, chunk, hps * dqk), lambda b, h, c: (b, c, h)),
            pl.BlockSpec((None, chunk, hps * dqk), lambda b, h, c: (b, c, k_off + h)),
            pl.BlockSpec((None, chunk, hps * dv), lambda b, h, c: (b, c, v_off + h)),
            pl.BlockSpec((None, chunk, hps * dv), lambda b, h, c: (b, c, o_off + h)),
            pl.BlockSpec((None, hps, None, 2, chunk), lambda b, h, c: (b, h, c, 0, 0)),
            pl.BlockSpec((1, hps * dv), lambda b, h, c: (0, h)),
            pl.BlockSpec((chunk, chunk), lambda b, h, c: (0, 0)),
        ],
        out_specs=pl.BlockSpec((None, chunk, hps * dv), lambda b, h, c: (b, c, h)),
        out_shape=jax.ShapeDtypeStruct((batch, seq, heads * dv), BF16),
        scratch_shapes=[
            pltpu.VMEM((hps, dqk, dv), F32),
            pltpu.VMEM((hps, 1, dqk), F32),
            pltpu.VMEM((hps, SUBLANE, LANE), F32),
        ],
        compiler_params=_params(("parallel", "parallel", "arbitrary"), 48),
        name="mlstm_recurrence",
    )(b_i, b_f, proj3, proj3, proj3, proj3, gf, head_g.reshape(1, heads * dv), tri)
    return out.reshape(batch * seq, heads * dv)


def _ffn_in_kernel(a_ref, wg_ref, wu_ref, cwg_ref, cwu_ref, cbg_ref, cbu_ref, o_ref, hg_ref, hu_ref,
                   *, bm, tiles_per_seq):
    first = (pl.program_id(1) % tiles_per_seq) == 0

    @pl.when(first)
    def _():
        hg_ref[0:SUBLANE, :] = jnp.zeros((SUBLANE, hg_ref.shape[1]), F32)
        hu_ref[0:SUBLANE, :] = jnp.zeros((SUBLANE, hu_ref.shape[1]), F32)

    @pl.when(jnp.logical_not(first))
    def _():
        hg_ref[0:SUBLANE, :] = hg_ref[bm:bm + SUBLANE, :]
        hu_ref[0:SUBLANE, :] = hu_ref[bm:bm + SUBLANE, :]

    a = a_ref[...]

    def conv_half(w_ref, cw_ref, cb_ref, h_ref):
        hcur = jnp.dot(a, w_ref[...], preferred_element_type=F32)
        h_ref[SUBLANE:SUBLANE + bm, :] = hcur
        hall = h_ref[0:SUBLANE + bm, :]
        h1 = pltpu.roll(hall, 1, axis=0)[SUBLANE:, :]
        h2 = pltpu.roll(hall, 2, axis=0)[SUBLANE:, :]
        cw = cw_ref[...]
        return cb_ref[...] + cw[0:1, :] * h2 + cw[1:2, :] * h1 + cw[2:3, :] * hcur

    g = conv_half(wg_ref, cwg_ref, cbg_ref, hg_ref)
    u = conv_half(wu_ref, cwu_ref, cbu_ref, hu_ref)
    o_ref[...] = (jax.nn.silu(g) * u).astype(o_ref.dtype)


def ffn_in(a, w_in, conv_w, conv_b, seq, bm_pref=1024, bn_pref=512):
    t, d = a.shape
    fp = w_in.shape[1] // 2
    bm, bn = _tile(seq, bm_pref), _tile(fp, bn_pref)
    nj = fp // bn
    return pl.pallas_call(
        functools.partial(_ffn_in_kernel, bm=bm, tiles_per_seq=seq // bm),
        grid=(nj, t // bm),
        in_specs=[
            pl.BlockSpec((bm, d), lambda j, i: (i, 0)),
            pl.BlockSpec((d, bn), lambda j, i: (0, j)),
            pl.BlockSpec((d, bn), lambda j, i: (0, nj + j)),
            pl.BlockSpec((FFN_CONV_WIDTH, bn), lambda j, i: (0, j)),
            pl.BlockSpec((FFN_CONV_WIDTH, bn), lambda j, i: (0, nj + j)),
            pl.BlockSpec((1, bn), lambda j, i: (0, j)),
            pl.BlockSpec((1, bn), lambda j, i: (0, nj + j)),
        ],
        out_specs=pl.BlockSpec((bm, bn), lambda j, i: (i, j)),
        out_shape=jax.ShapeDtypeStruct((t, fp), BF16),
        scratch_shapes=[pltpu.VMEM((bm + 2 * SUBLANE, bn), F32), pltpu.VMEM((bm + 2 * SUBLANE, bn), F32)],
        compiler_params=_params(("parallel", "arbitrary"), 56),
        name="ffn_in",
    )(a, w_in, w_in, conv_w, conv_w, conv_b, conv_b)


def _pool_kernel(x_ref, g_ref, w_ref, sc_ref, gn_ref, o_ref, on_ref, e_ref, p_ref, *, bm, tiles_per_seq, group_dim):
    step = pl.program_id(0)
    tile_in_seq = step % tiles_per_seq
    halo = POOL_HALO
    rows = halo + bm

    @pl.when(step == 0)
    def _():
        e_ref[0:halo, :] = jnp.zeros((halo, e_ref.shape[1]), F32)
        p_ref[0:SUBLANE, :] = jnp.zeros((SUBLANE, p_ref.shape[1]), F32)

    @pl.when(jnp.logical_and(step > 0, tile_in_seq == 0))
    def _():
        e_ref[0:halo, :] = jnp.zeros((halo, e_ref.shape[1]), F32)

    @pl.when(tile_in_seq > 0)
    def _():
        e_ref[0:halo, :] = e_ref[bm:bm + halo, :]

    x = x_ref[...]
    xn = x * lax.rsqrt(jnp.mean(x * x, axis=-1, keepdims=True) + RMS_EPS) * g_ref[...]
    e_ref[halo:rows, :] = xn

    pos = tile_in_seq * bm + lax.broadcasted_iota(jnp.int32, (bm, 1), 0)
    lo = SUBLANE
    for gi, window in enumerate(POOL_WINDOWS):
        cs = slice(gi * group_dim, (gi + 1) * group_dim)
        src, dst, span = e_ref, p_ref, 1
        while span < window:
            dst[lo:rows, cs] = src[lo:rows, cs] + src[lo - span:rows - span, cs]
            src, dst = dst, (e_ref if dst is p_ref else p_ref)
            span *= 2
        wsum = src[halo:rows, cs]
        cnt = jnp.minimum(pos + 1, window).astype(F32)
        pooled = wsum / cnt - xn[:, cs]
        y = jnp.dot(pooled.astype(BF16), w_ref[gi], preferred_element_type=F32)
        o_ref[:, cs] = x[:, cs] + y * sc_ref[:, cs]
        if window > 2:
            e_ref[halo:rows, cs] = xn[:, cs]

    y = o_ref[...]
    on_ref[...] = (y * lax.rsqrt(jnp.mean(y * y, axis=-1, keepdims=True) + RMS_EPS) * gn_ref[...]).astype(on_ref.dtype)


def pool_mixer(x2d, norm_g, w_group, scale, next_norm_g, seq, bm_pref=256):
    t, d = x2d.shape
    groups, group_dim, _ = w_group.shape
    bm = _tile(seq, bm_pref)
    return pl.pallas_call(
        functools.partial(_pool_kernel, bm=bm, tiles_per_seq=seq // bm, group_dim=group_dim),
        grid=(t // bm,),
        in_specs=[
            pl.BlockSpec((bm, d), lambda i: (i, 0)),
            pl.BlockSpec((1, d), lambda i: (0, 0)),
            pl.BlockSpec((groups, group_dim, group_dim), lambda i: (0, 0, 0)),
            pl.BlockSpec((1, d), lambda i: (0, 0)),
            pl.BlockSpec((1, d), lambda i: (0, 0)),
        ],
        out_specs=[pl.BlockSpec((bm, d), lambda i: (i, 0)), pl.BlockSpec((bm, d), lambda i: (i, 0))],
        out_shape=[jax.ShapeDtypeStruct((t, d), F32), jax.ShapeDtypeStruct((t, d), BF16)],
        scratch_shapes=[pltpu.VMEM((POOL_HALO + bm, d), F32), pltpu.VMEM((POOL_HALO + bm, d), F32)],
        compiler_params=_params(("arbitrary",), 56),
        name="pool_mixer",
    )(x2d, norm_g.reshape(1, d), w_group, scale.reshape(1, d), next_norm_g.reshape(1, d))


def _prep_w_in_kernel(w_ref, o_ref, *, d_ff, fp):
    rows = w_ref.shape[0]
    full = (d_ff // LANE) * LANE
    tail = d_ff - full
    x = w_ref[...]
    o_ref[:, 0:full] = x[:, 0:full].astype(BF16)
    if tail == 0:
        up = x[:, d_ff:2 * d_ff]
    else:
        keep = lax.broadcasted_iota(jnp.int32, (rows, LANE), 1) < tail
        o_ref[:, full:full + LANE] = jnp.where(keep, x[:, full:full + LANE], 0.0).astype(BF16)
        src = x[:, full:2 * d_ff]
        up = pltpu.roll(src, src.shape[1] - tail, axis=1)
        o_ref[:, fp + full:fp + full + LANE] = jnp.where(keep, up[:, full:full + LANE], 0.0).astype(BF16)
    o_ref[:, fp:fp + full] = up[:, 0:full].astype(BF16)
    used = full + (LANE if tail else 0)
    if used < fp:
        zeros = jnp.zeros((rows, fp - used), BF16)
        o_ref[:, used:fp] = zeros
        o_ref[:, fp + used:2 * fp] = zeros


def prep_w_in(w_in_all, layer, fp):
    _, d, two_ff = w_in_all.shape
    d_ff = two_ff // 2
    rows = _tile(d, 128)
    return pl.pallas_call(
        functools.partial(_prep_w_in_kernel, d_ff=d_ff, fp=fp),
        grid=(d // rows,),
        in_specs=[pl.BlockSpec((None, rows, two_ff), lambda i: (layer, i, 0))],
        out_specs=pl.BlockSpec((rows, 2 * fp), lambda i: (i, 0)),
        out_shape=jax.ShapeDtypeStruct((d, 2 * fp), BF16),
        compiler_params=_params(("parallel",), 48),
        name="prep_w_in",
    )(w_in_all)


def _prep_w_out_kernel(w_ref, o_ref, *, n_src):
    @pl.when(pl.program_id(0) < n_src)
    def _():
        o_ref[...] = w_ref[...].astype(BF16)

    @pl.when(pl.program_id(0) >= n_src)
    def _():
        o_ref[...] = jnp.zeros_like(o_ref)


def prep_w_out(w_out_all, layer, fp):
    _, d_ff, d = w_out_all.shape
    unit = 64
    assert d_ff % unit == 0
    units = d_ff // unit
    rows = unit * max(c for c in range(1, units + 1)
                      if units % c == 0 and c * unit * d * 4 <= 10 * MIB)
    n_src = d_ff // rows
    n_blocks = -(-fp // rows)
    return pl.pallas_call(
        functools.partial(_prep_w_out_kernel, n_src=n_src),
        grid=(n_blocks,),
        in_specs=[pl.BlockSpec((None, rows, d), lambda i: (layer, jnp.minimum(i, n_src - 1), 0))],
        out_specs=pl.BlockSpec((rows, d), lambda i: (i, 0)),
        out_shape=jax.ShapeDtypeStruct((n_blocks * rows, d), BF16),
        compiler_params=_params(("parallel",), 40),
        name="prep_w_out",
    )(w_out_all)


class FfnWeightPrep:
    MAX_BLOCK_BYTES = 4 * MIB

    def __init__(self, w_in_all, w_out_all, layer, fp):
        _, d, two_ff = w_in_all.shape
        self.d, self.d_ff, self.fp, self.layer = d, two_ff // 2, fp, layer
        self.arrays = (w_in_all, w_out_all)
        self.out_shapes = [jax.ShapeDtypeStruct((d, 2 * fp), BF16), jax.ShapeDtypeStruct((fp, d), BF16)]

    @staticmethod
    def _rows(total, unit, max_blocks):
        for c in range(1, total // unit + 1):
            if total % (c * unit) == 0 and total // (c * unit) <= max_blocks:
                return c * unit
        return None

    def plan(self, n_steps):
        unit_out = 16
        while self.d_ff % (unit_out * 2) == 0 and self.fp % (unit_out * 2) == 0:
            unit_out *= 2
        self.in_rows = self._rows(self.d, 16, n_steps)
        self.out_rows = self._rows(self.fp, unit_out, n_steps)
        if self.in_rows is None or self.out_rows is None or self.d_ff % self.out_rows:
            return False
        self.n_in, self.n_src, self.n_out = self.d // self.in_rows, self.d_ff // self.out_rows, self.fp // self.out_rows
        return max(self.in_rows * 2 * self.d_ff, self.out_rows * self.d) * 4 <= self.MAX_BLOCK_BYTES

    def specs(self, step):
        layer, n_in, n_src, n_out = self.layer, self.n_in, self.n_src, self.n_out
        ins = [pl.BlockSpec((None, self.in_rows, 2 * self.d_ff),
                            lambda *g: (layer, jnp.minimum(step(*g), n_in - 1), 0)),
               pl.BlockSpec((None, self.out_rows, self.d),
                            lambda *g: (layer, jnp.minimum(step(*g), n_src - 1), 0))]
        outs = [pl.BlockSpec((self.in_rows, 2 * self.fp), lambda *g: (jnp.minimum(step(*g), n_in - 1), 0)),
                pl.BlockSpec((self.out_rows, self.d), lambda *g: (jnp.minimum(step(*g), n_out - 1), 0))]
        return ins, outs

    def body(self, step, in_refs, out_refs):
        _prep_w_in_kernel(in_refs[0], out_refs[0], d_ff=self.d_ff, fp=self.fp)
        real = jnp.minimum(step, self.n_out - 1) < self.n_src
        out_refs[1][...] = jnp.where(real, in_refs[1][...], 0.0).astype(BF16)


def _pad_cols(w, width):
    return jnp.pad(w, [(0, 0)] * (w.ndim - 1) + [(0, width - w.shape[-1])])


FFN_OUT_K_STEPS = 4


def _padded_hidden(d_ff):
    unit = 1024 if d_ff > 1024 else LANE * FFN_OUT_K_STEPS
    return -(-d_ff // unit) * unit


def _conv_ffn(x2d, xn, w_in_all, conv_w, conv_b, w_out_all, layer, seq, prepared=None, side=None):
    d_ff = w_out_all.shape[1]
    fp = _padded_hidden(d_ff)
    w_in_p, w_out_p = prepared or (prep_w_in(w_in_all, layer, fp), prep_w_out(w_out_all, layer, fp))
    conv_w_p = _pad_cols(conv_w.reshape(FFN_CONV_WIDTH, 2, d_ff), fp).reshape(FFN_CONV_WIDTH, 2 * fp)
    conv_b_p = _pad_cols(conv_b.reshape(1, 2, d_ff), fp).reshape(1, 2 * fp)
    act = ffn_in(xn, w_in_p, conv_w_p, conv_b_p, seq)
    return matmul_residual(act, w_out_p, x2d, FFN_OUT_K_STEPS, side=side)


def _mlstm_layer(x2d, norm_g, w_in, b_i, b_f, head_g, w_out, batch, seq, side=None):
    heads = b_i.shape[0]
    main = w_in.shape[1] - 2 * heads
    xn = rmsnorm(x2d, norm_g, BF16)
    proj, side_out = matmul(xn, w_in.astype(BF16), BF16, n=main, side=side)
    gates, _ = matmul(xn, _pad_cols(w_in[:, main:], LANE).astype(BF16), F32)
    hgated = mlstm_recurrence(proj, gates, b_i, b_f, head_g, batch, seq)
    out, _ = matmul_residual(hgated, w_out.astype(BF16), x2d, 1)
    return out, side_out


def kernel(x, mlstm_norm_g, mlstm_w_in, mlstm_b_i, mlstm_b_f, mlstm_head_g, mlstm_w_out,
           pool_norm_g, pool_w_group, pool_scale,
           ffn_norm_g, ffn_w_in, ffn_conv_w, ffn_conv_b, ffn_w_out, final_norm_g):
    batch, seq, d = x.shape
    depth = ffn_w_in.shape[0]
    fp = _padded_hidden(ffn_w_out.shape[1])
    x2d = x.reshape(batch * seq, d)
    prepared = {}
    for layer in range(depth):
        j = layer // 2
        if layer % 2 == 0:
            x2d, prepared[layer] = _mlstm_layer(x2d, mlstm_norm_g[j], mlstm_w_in[j], mlstm_b_i[j], mlstm_b_f[j],
                                                mlstm_head_g[j], mlstm_w_out[j], batch, seq,
                                                side=FfnWeightPrep(ffn_w_in, ffn_w_out, layer, fp))
            xn = rmsnorm(x2d, ffn_norm_g[layer], BF16)
            side = FfnWeightPrep(ffn_w_in, ffn_w_out, layer + 1, fp) if layer + 1 < depth else None
        else:
            x2d, xn = pool_mixer(x2d, pool_norm_g[j], pool_w_group[j].astype(BF16), pool_scale[j],
                                 ffn_norm_g[layer], seq)
            side = None
        x2d, prepared[layer + 1] = _conv_ffn(x2d, xn, ffn_w_in, ffn_conv_w[layer], ffn_conv_b[layer], ffn_w_out,
                                             layer, seq, prepared=prepared.get(layer), side=side)
    return rmsnorm(x2d, final_norm_g, x.dtype).reshape(batch, seq, d)
```

```python
import functools

import jax
import jax.numpy as jnp
from jax import lax
from jax.experimental import pallas as pl
from jax.experimental.pallas import tpu as pltpu

RMS_EPS = 1e-6
GATE_SOFTCAP = 15.0
POOL_WINDOWS = (2, 4, 8, 16)
FFN_CONV_WIDTH = 3
MLSTM_CHUNK = 256
MLSTM_HEADS_PER_STEP = 2
FFN_IN_COLS = 512
LANE = 128
SUBLANE = 8
POOL_HALO = 32
MIB = 1024 * 1024

F32 = jnp.float32
BF16 = jnp.bfloat16


def _params(semantics, vmem_mib):
    return pltpu.CompilerParams(dimension_semantics=semantics, vmem_limit_bytes=vmem_mib * MIB)


def _tile(dim, pref):
    t = min(dim, pref)
    while dim % t:
        t //= 2
    return t


def _rmsnorm_kernel(x_ref, g_ref, o_ref):
    x = x_ref[...]
    y = x * lax.rsqrt(jnp.mean(x * x, axis=-1, keepdims=True) + RMS_EPS)
    o_ref[...] = (y * g_ref[...]).astype(o_ref.dtype)


def rmsnorm(x2d, g, out_dtype):
    t, d = x2d.shape
    bm = _tile(t, 512)
    return pl.pallas_call(
        _rmsnorm_kernel,
        grid=(t // bm,),
        in_specs=[pl.BlockSpec((bm, d), lambda i: (i, 0)), pl.BlockSpec((1, d), lambda i: (0, 0))],
        out_specs=pl.BlockSpec((bm, d), lambda i: (i, 0)),
        out_shape=jax.ShapeDtypeStruct((t, d), out_dtype),
        compiler_params=_params(("parallel",), 48),
        name="rmsnorm",
    )(x2d, g.reshape(1, d))


def _mm_kernel(a_ref, w_ref, *refs, side, n_inner):
    n_side_in = len(side.arrays) if side else 0
    o_ref = refs[n_side_in]
    o_ref[...] = jnp.dot(a_ref[...], w_ref[...], preferred_element_type=F32).astype(o_ref.dtype)
    if side:
        side.body(pl.program_id(0) * n_inner + pl.program_id(1), refs[:n_side_in], refs[n_side_in + 1:])


def matmul(a, w, out_dtype, n=None, side=None, bm_pref=1024, bn_pref=1024):
    m, k = a.shape
    n = w.shape[1] if n is None else n
    bm, bn = _tile(m, bm_pref), _tile(n, bn_pref)
    ni = m // bm
    if side and not side.plan((n // bn) * ni):
        side = None
    step = lambda j, i: j * ni + i
    side_in, side_out = side.specs(step) if side else ([], [])
    outs = pl.pallas_call(
        functools.partial(_mm_kernel, side=side, n_inner=ni),
        grid=(n // bn, ni),
        in_specs=[pl.BlockSpec((bm, k), lambda j, i: (i, 0)), pl.BlockSpec((k, bn), lambda j, i: (0, j))] + side_in,
        out_specs=[pl.BlockSpec((bm, bn), lambda j, i: (i, j))] + side_out,
        out_shape=[jax.ShapeDtypeStruct((m, n), out_dtype)] + (side.out_shapes if side else []),
        compiler_params=_params(("parallel", "arbitrary" if side else "parallel"), 60 if side else 56),
        name="matmul",
    )(a, w, *(side.arrays if side else ()))
    return outs[0], (tuple(outs[1:]) if side else None)


def _mm_res_kernel(a_ref, w_ref, x_ref, *refs, nk, side):
    n_side_in = len(side.arrays) if side else 0
    o_ref = refs[n_side_in]

    def partial_product():
        if side:
            step = (pl.program_id(0) * pl.num_programs(1) + pl.program_id(1)) * nk + pl.program_id(2)
            side.body(step, refs[:n_side_in], refs[n_side_in + 1:])
        return jnp.dot(a_ref[...], w_ref[...], preferred_element_type=F32)

    if nk == 1:
        o_ref[...] = x_ref[...] + partial_product()
        return
    kk = pl.program_id(2)

    @pl.when(kk == 0)
    def _():
        o_ref[...] = partial_product()

    @pl.when(jnp.logical_and(kk > 0, kk < nk - 1))
    def _():
        o_ref[...] += partial_product()

    @pl.when(kk == nk - 1)
    def _():
        o_ref[...] = x_ref[...] + (o_ref[...] + partial_product())


def matmul_residual(a, w, x, nk, side=None, bm_pref=1024, bn_pref=1024):
    m, k = a.shape
    n = w.shape[1]
    bm, bn, bk = _tile(m, bm_pref), _tile(n, bn_pref), k // nk
    assert bk * nk == k and (nk == 1 or bk % LANE == 0)
    nj = n // bn
    if side and not side.plan((m // bm) * nj * nk):
        side = None
    step = lambda i, j, kk: (i * nj + j) * nk + kk
    side_in, side_out = side.specs(step) if side else ([], [])
    outs = pl.pallas_call(
        functools.partial(_mm_res_kernel, nk=nk, side=side),
        grid=(m // bm, nj, nk),
        in_specs=[
            pl.BlockSpec((bm, bk), lambda i, j, kk: (i, kk)),
            pl.BlockSpec((bk, bn), lambda i, j, kk: (kk, j)),
            pl.BlockSpec((bm, bn), lambda i, j, kk: (i, j)),
        ] + side_in,
        out_specs=[pl.BlockSpec((bm, bn), lambda i, j, kk: (i, j))] + side_out,
        out_shape=[jax.ShapeDtypeStruct((m, n), F32)] + (side.out_shapes if side else []),
        compiler_params=_params(("arbitrary",) * 3 if side else ("parallel", "parallel", "arbitrary"), 56),
        name="matmul_residual",
    )(a, w, x, *(side.arrays if side else ()))
    return outs[0], (tuple(outs[1:]) if side else None)


def _as_column(row, length):
    return jnp.transpose(jnp.broadcast_to(row, (LANE, length)))[:, 0:1]


def _cumsum_lanes(row, tri):
    x = jnp.broadcast_to(row, (SUBLANE, row.shape[1]))
    hi = x.astype(BF16).astype(F32)
    r1 = x - hi
    mid = r1.astype(BF16).astype(F32)
    lo = (r1 - mid).astype(BF16).astype(F32)
    parts = jnp.concatenate([hi, mid, lo], axis=0).astype(BF16)
    acc = jnp.dot(parts, tri, preferred_element_type=F32)
    s = acc[0:SUBLANE] + acc[SUBLANE:2 * SUBLANE] + acc[2 * SUBLANE:3 * SUBLANE]
    return s[0:1, :]


def _mlstm_kernel(bi_ref, bf_ref, q_ref, k_ref, v_ref, o_ref, gf_ref, hg_ref, tri_ref, out_ref,
                  c_ref, n_ref, m_ref, *, chunk, dqk, dv, heads_per_step, qk_scale):
    @pl.when(pl.program_id(2) == 0)
    def _():
        c_ref[...] = jnp.zeros_like(c_ref)
        n_ref[...] = jnp.zeros_like(n_ref)
        m_ref[...] = jnp.zeros_like(m_ref)

    tri = tri_ref[...]
    t_idx = lax.broadcasted_iota(jnp.int32, (chunk, chunk), 0)
    s_idx = lax.broadcasted_iota(jnp.int32, (chunk, chunk), 1)
    causal = t_idx >= s_idx

    for hh in range(heads_per_step):
        h = pl.program_id(1) * heads_per_step + hh
        qs = slice(hh * dqk, (hh + 1) * dqk)
        vs = slice(hh * dv, (hh + 1) * dv)
        gf = gf_ref[hh]
        i_row = GATE_SOFTCAP * jnp.tanh((gf[0:1, :] + bi_ref[h]) / GATE_SOFTCAP)
        logf = jax.nn.log_sigmoid(GATE_SOFTCAP * jnp.tanh((gf[1:2, :] + bf_ref[h]) / GATE_SOFTCAP))
        b_row = _cumsum_lanes(logf, tri)
        d_row = i_row - b_row
        b_col = _as_column(b_row, chunk)
        d_col = _as_column(d_row, chunk)

        log_intra = jnp.where(causal, b_col + d_row, -jnp.inf)
        m_prev = m_ref[hh, 0:1, 0:1]
        log_inter = b_col + m_prev
        m_t = jnp.maximum(log_inter, jnp.max(log_intra, axis=1, keepdims=True))
        w_inter = jnp.exp(log_inter - m_t)
        w_intra = jnp.exp(log_intra - m_t)

        q = q_ref[:, qs] * qk_scale
        k = k_ref[:, qs]
        v = v_ref[:, vs]
        s = lax.dot_general(q, k, (((1,), (1,)), ((), ())), preferred_element_type=F32) * w_intra
        c_prev = c_ref[hh]
        n_prev = n_ref[hh]
        num = (w_inter * jnp.dot(q, c_prev.astype(BF16), preferred_element_type=F32)
               + jnp.dot(s.astype(BF16), v, preferred_element_type=F32))
        qn = jnp.sum(q.astype(F32) * n_prev, axis=1, keepdims=True)
        den = w_inter * qn + jnp.sum(s, axis=1, keepdims=True)
        hval = num / jnp.maximum(jnp.abs(den), jnp.exp(-m_t))
        hval = hval * lax.rsqrt(jnp.mean(hval * hval, axis=-1, keepdims=True) + RMS_EPS)
        hval = hval * hg_ref[:, vs]
        out_ref[:, vs] = (hval * jax.nn.sigmoid(o_ref[:, vs].astype(F32))).astype(out_ref.dtype)

        b_last = b_row[:, chunk - 1:chunk]
        log_prev = b_last + m_prev
        m_new = jnp.maximum(log_prev, jnp.max(b_last + d_row, axis=1, keepdims=True))
        a_prev = jnp.exp(log_prev - m_new)
        ka = jnp.exp(b_last + d_col - m_new) * k.astype(F32)
        c_ref[hh] = a_prev * c_prev + lax.dot_general(
            ka.astype(BF16), v, (((0,), (0,)), ((), ())), preferred_element_type=F32)
        n_ref[hh] = a_prev * n_prev + jnp.sum(ka, axis=0, keepdims=True)
        m_ref[hh] = jnp.broadcast_to(m_new, m_ref.shape[1:])


def mlstm_recurrence(proj, gates, b_i, b_f, head_g, batch, seq):
    heads = b_i.shape[0]
    dv = head_g.shape[0] // heads
    dqk = dv // 2
    chunk = _tile(seq, MLSTM_CHUNK)
    nc = seq // chunk
    hps = _tile(heads, MLSTM_HEADS_PER_STEP)
    proj3 = proj.reshape(batch, seq, proj.shape[1])
    gf = gates[:, :2 * heads].reshape(batch, nc, chunk, 2, heads).transpose(0, 4, 1, 3, 2)
    tri = (jnp.arange(chunk)[:, None] <= jnp.arange(chunk)[None, :]).astype(BF16)
    k_off = heads // hps
    v_off = 2 * heads * dqk // (dv * hps)
    o_off = v_off + heads // hps
    smem = pl.BlockSpec(memory_space=pltpu.SMEM)
    out = pl.pallas_call(
        functools.partial(_mlstm_kernel, chunk=chunk, dqk=dqk, dv=dv, heads_per_step=hps,
                          qk_scale=float(dqk) ** -0.5),
        grid=(batch, heads // hps, nc),
        in_specs=[
            smem, smem,
            pl.BlockSpec((None, chunk, hps * dqk), lambda b, h, c: (b, c, h)),
            pl.BlockSpec((None, chunk, hps * dqk), lambda b, h, c: (b, c, k_off + h)),
            pl.BlockSpec((None, chunk, hps * dv), lambda b, h, c: (b, c, v_off + h)),
            pl.BlockSpec((None, chunk, hps * dv), lambda b, h, c: (b, c, o_off + h)),
            pl.BlockSpec((None, hps, None, 2, chunk), lambda b, h, c: (b, h, c, 0, 0)),
            pl.BlockSpec((1, hps * dv), lambda b, h, c: (0, h)),
            pl.BlockSpec((chunk, chunk), lambda b, h, c: (0, 0)),
        ],
        out_specs=pl.BlockSpec((None, chunk, hps * dv), lambda b, h, c: (b, c, h)),
        out_shape=jax.ShapeDtypeStruct((batch, seq, heads * dv), BF16),
        scratch_shapes=[
            pltpu.VMEM((hps, dqk, dv), F32),
            pltpu.VMEM((hps, 1, dqk), F32),
            pltpu.VMEM((hps, SUBLANE, LANE), F32),
        ],
        compiler_params=_params(("parallel", "parallel", "arbitrary"), 48),
        name="mlstm_recurrence",
    )(b_i, b_f, proj3, proj3, proj3, proj3, gf, head_g.reshape(1, heads * dv), tri)
    return out.reshape(batch * seq, heads * dv)


def _ffn_in_kernel(a_ref, wg_ref, wu_ref, cwg_ref, cwu_ref, cbg_ref, cbu_ref, o_ref, hg_ref, hu_ref,
                   *, bm, tiles_per_seq, last_width):
    first = (pl.program_id(1) % tiles_per_seq) == 0

    @pl.when(first)
    def _():
        hg_ref[0:SUBLANE, :] = jnp.zeros((SUBLANE, hg_ref.shape[1]), F32)
        hu_ref[0:SUBLANE, :] = jnp.zeros((SUBLANE, hu_ref.shape[1]), F32)

    @pl.when(jnp.logical_not(first))
    def _():
        hg_ref[0:SUBLANE, :] = hg_ref[bm:bm + SUBLANE, :]
        hu_ref[0:SUBLANE, :] = hu_ref[bm:bm + SUBLANE, :]

    def tile(width):
        cs = slice(0, width)
        a = a_ref[...]

        def conv_half(w_ref, cw_ref, cb_ref, h_ref):
            hcur = jnp.dot(a, w_ref[:, cs], preferred_element_type=F32)
            h_ref[SUBLANE:SUBLANE + bm, cs] = hcur
            hall = h_ref[0:SUBLANE + bm, cs]
            h1 = pltpu.roll(hall, 1, axis=0)[SUBLANE:, :]
            h2 = pltpu.roll(hall, 2, axis=0)[SUBLANE:, :]
            cw = cw_ref[:, cs]
            return cb_ref[:, cs] + cw[0:1, :] * h2 + cw[1:2, :] * h1 + cw[2:3, :] * hcur

        g = conv_half(wg_ref, cwg_ref, cbg_ref, hg_ref)
        u = conv_half(wu_ref, cwu_ref, cbu_ref, hu_ref)
        o_ref[:, cs] = (jax.nn.silu(g) * u).astype(o_ref.dtype)

    bn = o_ref.shape[1]
    if last_width == bn:
        tile(bn)
    else:
        last = pl.program_id(0) == pl.num_programs(0) - 1

        @pl.when(jnp.logical_not(last))
        def _():
            tile(bn)

        @pl.when(last)
        def _():
            tile(last_width)


def ffn_in(a, w_in, conv_w, conv_b, width, seq, bm_pref=1024):
    t, d = a.shape
    half = w_in.shape[1] // 2
    bm, bn = _tile(seq, bm_pref), min(FFN_IN_COLS, half)
    nj = half // bn
    assert nj * bn == half and (nj - 1) * bn < width <= half and width % LANE == 0
    return pl.pallas_call(
        functools.partial(_ffn_in_kernel, bm=bm, tiles_per_seq=seq // bm, last_width=width - (nj - 1) * bn),
        grid=(nj, t // bm),
        in_specs=[
            pl.BlockSpec((bm, d), lambda j, i: (i, 0)),
            pl.BlockSpec((d, bn), lambda j, i: (0, j)),
            pl.BlockSpec((d, bn), lambda j, i: (0, nj + j)),
            pl.BlockSpec((FFN_CONV_WIDTH, bn), lambda j, i: (0, j)),
            pl.BlockSpec((FFN_CONV_WIDTH, bn), lambda j, i: (0, nj + j)),
            pl.BlockSpec((1, bn), lambda j, i: (0, j)),
            pl.BlockSpec((1, bn), lambda j, i: (0, nj + j)),
        ],
        out_specs=pl.BlockSpec((bm, bn), lambda j, i: (i, j)),
        out_shape=jax.ShapeDtypeStruct((t, width), BF16),
        scratch_shapes=[pltpu.VMEM((bm + 2 * SUBLANE, bn), F32), pltpu.VMEM((bm + 2 * SUBLANE, bn), F32)],
        compiler_params=_params(("parallel", "arbitrary"), 56),
        name="ffn_in",
    )(a, w_in, w_in, conv_w, conv_w, conv_b, conv_b)


def _pool_kernel(x_ref, g_ref, w_ref, sc_ref, gn_ref, o_ref, on_ref, e_ref, p_ref, *, bm, tiles_per_seq, group_dim):
    step = pl.program_id(0)
    tile_in_seq = step % tiles_per_seq
    halo = POOL_HALO
    rows = halo + bm

    @pl.when(step == 0)
    def _():
        e_ref[0:halo, :] = jnp.zeros((halo, e_ref.shape[1]), F32)
        p_ref[0:SUBLANE, :] = jnp.zeros((SUBLANE, p_ref.shape[1]), F32)

    @pl.when(jnp.logical_and(step > 0, tile_in_seq == 0))
    def _():
        e_ref[0:halo, :] = jnp.zeros((halo, e_ref.shape[1]), F32)

    @pl.when(tile_in_seq > 0)
    def _():
        e_ref[0:halo, :] = e_ref[bm:bm + halo, :]

    x = x_ref[...]
    xn = x * lax.rsqrt(jnp.mean(x * x, axis=-1, keepdims=True) + RMS_EPS) * g_ref[...]
    e_ref[halo:rows, :] = xn

    pos = tile_in_seq * bm + lax.broadcasted_iota(jnp.int32, (bm, 1), 0)
    lo = SUBLANE
    for gi, window in enumerate(POOL_WINDOWS):
        cs = slice(gi * group_dim, (gi + 1) * group_dim)
        src, dst, span = e_ref, p_ref, 1
        while span < window:
            dst[lo:rows, cs] = src[lo:rows, cs] + src[lo - span:rows - span, cs]
            src, dst = dst, (e_ref if dst is p_ref else p_ref)
            span *= 2
        wsum = src[halo:rows, cs]
        cnt = jnp.minimum(pos + 1, window).astype(F32)
        pooled = wsum / cnt - xn[:, cs]
        y = jnp.dot(pooled.astype(BF16), w_ref[gi], preferred_element_type=F32)
        o_ref[:, cs] = x[:, cs] + y * sc_ref[:, cs]
        if window > 2:
            e_ref[halo:rows, cs] = xn[:, cs]

    y = o_ref[...]
    on_ref[...] = (y * lax.rsqrt(jnp.mean(y * y, axis=-1, keepdims=True) + RMS_EPS) * gn_ref[...]).astype(on_ref.dtype)


def pool_mixer(x2d, norm_g, w_group, scale, next_norm_g, seq, bm_pref=256):
    t, d = x2d.shape
    groups, group_dim, _ = w_group.shape
    bm = _tile(seq, bm_pref)
    return pl.pallas_call(
        functools.partial(_pool_kernel, bm=bm, tiles_per_seq=seq // bm, group_dim=group_dim),
        grid=(t // bm,),
        in_specs=[
            pl.BlockSpec((bm, d), lambda i: (i, 0)),
            pl.BlockSpec((1, d), lambda i: (0, 0)),
            pl.BlockSpec((groups, group_dim, group_dim), lambda i: (0, 0, 0)),
            pl.BlockSpec((1, d), lambda i: (0, 0)),
            pl.BlockSpec((1, d), lambda i: (0, 0)),
        ],
        out_specs=[pl.BlockSpec((bm, d), lambda i: (i, 0)), pl.BlockSpec((bm, d), lambda i: (i, 0))],
        out_shape=[jax.ShapeDtypeStruct((t, d), F32), jax.ShapeDtypeStruct((t, d), BF16)],
        scratch_shapes=[pltpu.VMEM((POOL_HALO + bm, d), F32), pltpu.VMEM((POOL_HALO + bm, d), F32)],
        compiler_params=_params(("arbitrary",), 56),
        name="pool_mixer",
    )(x2d, norm_g.reshape(1, d), w_group, scale.reshape(1, d), next_norm_g.reshape(1, d))


def _prep_w_in_kernel(w_ref, o_ref, *, d_ff, fp):
    rows = w_ref.shape[0]
    full = (d_ff // LANE) * LANE
    tail = d_ff - full
    x = w_ref[...]
    o_ref[:, 0:full] = x[:, 0:full].astype(BF16)
    if tail == 0:
        up = x[:, d_ff:2 * d_ff]
    else:
        keep = lax.broadcasted_iota(jnp.int32, (rows, LANE), 1) < tail
        o_ref[:, full:full + LANE] = jnp.where(keep, x[:, full:full + LANE], 0.0).astype(BF16)
        src = x[:, full:2 * d_ff]
        up = pltpu.roll(src, src.shape[1] - tail, axis=1)
        o_ref[:, fp + full:fp + full + LANE] = jnp.where(keep, up[:, full:full + LANE], 0.0).astype(BF16)
    o_ref[:, fp:fp + full] = up[:, 0:full].astype(BF16)
    used = full + (LANE if tail else 0)
    if used < fp:
        zeros = jnp.zeros((rows, fp - used), BF16)
        o_ref[:, used:fp] = zeros
        o_ref[:, fp + used:2 * fp] = zeros


def prep_w_in(w_in_all, layer, fp):
    _, d, two_ff = w_in_all.shape
    d_ff = two_ff // 2
    rows = _tile(d, 128)
    return pl.pallas_call(
        functools.partial(_prep_w_in_kernel, d_ff=d_ff, fp=fp),
        grid=(d // rows,),
        in_specs=[pl.BlockSpec((None, rows, two_ff), lambda i: (layer, i, 0))],
        out_specs=pl.BlockSpec((rows, 2 * fp), lambda i: (i, 0)),
        out_shape=jax.ShapeDtypeStruct((d, 2 * fp), BF16),
        compiler_params=_params(("parallel",), 48),
        name="prep_w_in",
    )(w_in_all)


def _prep_w_out_kernel(w_ref, o_ref, *, n_src):
    @pl.when(pl.program_id(0) < n_src)
    def _():
        o_ref[...] = w_ref[...].astype(BF16)

    @pl.when(pl.program_id(0) >= n_src)
    def _():
        o_ref[...] = jnp.zeros_like(o_ref)


def prep_w_out(w_out_all, layer, fp):
    _, d_ff, d = w_out_all.shape
    unit = 64
    assert d_ff % unit == 0
    units = d_ff // unit
    rows = unit * max(c for c in range(1, units + 1)
                      if units % c == 0 and c * unit * d * 4 <= 10 * MIB)
    n_src = d_ff // rows
    n_blocks = -(-fp // rows)
    return pl.pallas_call(
        functools.partial(_prep_w_out_kernel, n_src=n_src),
        grid=(n_blocks,),
        in_specs=[pl.BlockSpec((None, rows, d), lambda i: (layer, jnp.minimum(i, n_src - 1), 0))],
        out_specs=pl.BlockSpec((rows, d), lambda i: (i, 0)),
        out_shape=jax.ShapeDtypeStruct((n_blocks * rows, d), BF16),
        compiler_params=_params(("parallel",), 40),
        name="prep_w_out",
    )(w_out_all)


class FfnWeightPrep:
    MAX_BLOCK_BYTES = 4 * MIB

    def __init__(self, w_in_all, w_out_all, layer, fp):
        _, d, two_ff = w_in_all.shape
        self.d, self.d_ff, self.fp, self.layer = d, two_ff // 2, fp, layer
        self.arrays = (w_in_all, w_out_all)
        self.out_shapes = [jax.ShapeDtypeStruct((d, 2 * fp), BF16), jax.ShapeDtypeStruct((fp, d), BF16)]

    @staticmethod
    def _rows(total, unit, max_blocks):
        for c in range(1, total // unit + 1):
            if total % (c * unit) == 0 and total // (c * unit) <= max_blocks:
                return c * unit
        return None

    def plan(self, n_steps):
        unit_out = 16
        while self.d_ff % (unit_out * 2) == 0 and self.fp % (unit_out * 2) == 0:
            unit_out *= 2
        self.in_rows = self._rows(self.d, 16, n_steps)
        self.out_rows = self._rows(self.fp, unit_out, n_steps)
        if self.in_rows is None or self.out_rows is None or self.d_ff % self.out_rows:
            return False
        self.n_in, self.n_src, self.n_out = self.d // self.in_rows, self.d_ff // self.out_rows, self.fp // self.out_rows
        return max(self.in_rows * 2 * self.d_ff, self.out_rows * self.d) * 4 <= self.MAX_BLOCK_BYTES

    def specs(self, step):
        layer, n_in, n_src, n_out = self.layer, self.n_in, self.n_src, self.n_out
        ins = [pl.BlockSpec((None, self.in_rows, 2 * self.d_ff),
                            lambda *g: (layer, jnp.minimum(step(*g), n_in - 1), 0)),
               pl.BlockSpec((None, self.out_rows, self.d),
                            lambda *g: (layer, jnp.minimum(step(*g), n_src - 1), 0))]
        outs = [pl.BlockSpec((self.in_rows, 2 * self.fp), lambda *g: (jnp.minimum(step(*g), n_in - 1), 0)),
                pl.BlockSpec((self.out_rows, self.d), lambda *g: (jnp.minimum(step(*g), n_out - 1), 0))]
        return ins, outs

    def body(self, step, in_refs, out_refs):
        _prep_w_in_kernel(in_refs[0], out_refs[0], d_ff=self.d_ff, fp=self.fp)
        real = jnp.minimum(step, self.n_out - 1) < self.n_src
        out_refs[1][...] = jnp.where(real, in_refs[1][...], 0.0).astype(BF16)


def _pad_cols(w, width):
    return jnp.pad(w, [(0, 0)] * (w.ndim - 1) + [(0, width - w.shape[-1])])


FFN_OUT_K_STEPS = 2
MXU_TILE = 256


def _padded_hidden(d_ff):
    unit = max(MXU_TILE, LANE * FFN_OUT_K_STEPS)
    width = -(-d_ff // unit) * unit
    cols = min(FFN_IN_COLS, width)
    return width, -(-width // cols) * cols


def _conv_ffn(x2d, xn, w_in_all, conv_w, conv_b, w_out_all, layer, seq, prepared=None, side=None):
    d_ff = w_out_all.shape[1]
    width, stride = _padded_hidden(d_ff)
    w_in_p, w_out_p = prepared or (prep_w_in(w_in_all, layer, stride), prep_w_out(w_out_all, layer, stride))
    conv_w_p = _pad_cols(conv_w.reshape(FFN_CONV_WIDTH, 2, d_ff), stride).reshape(FFN_CONV_WIDTH, 2 * stride)
    conv_b_p = _pad_cols(conv_b.reshape(1, 2, d_ff), stride).reshape(1, 2 * stride)
    act = ffn_in(xn, w_in_p, conv_w_p, conv_b_p, width, seq)
    return matmul_residual(act, w_out_p, x2d, FFN_OUT_K_STEPS, side=side, bn_pref=512)


def _mlstm_layer(x2d, norm_g, w_in, b_i, b_f, head_g, w_out, batch, seq, side=None):
    heads = b_i.shape[0]
    main = w_in.shape[1] - 2 * heads
    xn = rmsnorm(x2d, norm_g, BF16)
    proj, side_out = matmul(xn, w_in.astype(BF16), BF16, n=main, side=side)
    gates, _ = matmul(xn, _pad_cols(w_in[:, main:], LANE).astype(BF16), F32)
    hgated = mlstm_recurrence(proj, gates, b_i, b_f, head_g, batch, seq)
    out, _ = matmul_residual(hgated, w_out.astype(BF16), x2d, 1)
    return out, side_out


def kernel(x, mlstm_norm_g, mlstm_w_in, mlstm_b_i, mlstm_b_f, mlstm_head_g, mlstm_w_out,
           pool_norm_g, pool_w_group, pool_scale,
           ffn_norm_g, ffn_w_in, ffn_conv_w, ffn_conv_b, ffn_w_out, final_norm_g):
    batch, seq, d = x.shape
    depth = ffn_w_in.shape[0]
    _, fp = _padded_hidden(ffn_w_out.shape[1])
    x2d = x.reshape(batch * seq, d)
    prepared = {}
    for layer in range(depth):
        j = layer // 2
        if layer % 2 == 0:
            x2d, prepared[layer] = _mlstm_layer(x2d, mlstm_norm_g[j], mlstm_w_in[j], mlstm_b_i[j], mlstm_b_f[j],
                                                mlstm_head_g[j], mlstm_w_out[j], batch, seq,
                                                side=FfnWeightPrep(ffn_w_in, ffn_w_out, layer, fp))
            xn = rmsnorm(x2d, ffn_norm_g[layer], BF16)
            side = FfnWeightPrep(ffn_w_in, ffn_w_out, layer + 1, fp) if layer + 1 < depth else None
        else:
            x2d, xn = pool_mixer(x2d, pool_norm_g[j], pool_w_group[j].astype(BF16), pool_scale[j],
                                 ffn_norm_g[layer], seq)
            side = None
        x2d, prepared[layer + 1] = _conv_ffn(x2d, xn, ffn_w_in, ffn_conv_w[layer], ffn_conv_b[layer], ffn_w_out,
                                             layer, seq, prepared=prepared.get(layer), side=side)
    return rmsnorm(x2d, final_norm_g, x.dtype).reshape(batch, seq, d)
```

```python
import functools

import jax
import jax.numpy as jnp
from jax import lax
from jax.experimental import pallas as pl
from jax.experimental.pallas import tpu as pltpu

RMS_EPS = 1e-6
GATE_SOFTCAP = 15.0
POOL_WINDOWS = (2, 4, 8, 16)
FFN_CONV_WIDTH = 3
MLSTM_CHUNK = 256
MLSTM_HEADS_PER_STEP = 2
FFN_IN_COLS = 512
LANE = 128
SUBLANE = 8
POOL_HALO = 32
MIB = 1024 * 1024

F32 = jnp.float32
BF16 = jnp.bfloat16


def _params(semantics, vmem_mib):
    return pltpu.CompilerParams(dimension_semantics=semantics, vmem_limit_bytes=vmem_mib * MIB)


def _tile(dim, pref):
    t = min(dim, pref)
    while dim % t:
        t //= 2
    return t


def _rmsnorm_kernel(x_ref, g_ref, o_ref):
    x = x_ref[...]
    y = x * lax.rsqrt(jnp.mean(x * x, axis=-1, keepdims=True) + RMS_EPS)
    o_ref[...] = (y * g_ref[...]).astype(o_ref.dtype)


def rmsnorm(x2d, g, out_dtype):
    t, d = x2d.shape
    bm = _tile(t, 512)
    return pl.pallas_call(
        _rmsnorm_kernel,
        grid=(t // bm,),
        in_specs=[pl.BlockSpec((bm, d), lambda i: (i, 0)), pl.BlockSpec((1, d), lambda i: (0, 0))],
        out_specs=pl.BlockSpec((bm, d), lambda i: (i, 0)),
        out_shape=jax.ShapeDtypeStruct((t, d), out_dtype),
        compiler_params=_params(("parallel",), 48),
        name="rmsnorm",
    )(x2d, g.reshape(1, d))


def _mm_kernel(a_ref, w_ref, *refs, side, n_inner):
    n_side_in = len(side.arrays) if side else 0
    o_ref = refs[n_side_in]
    o_ref[...] = jnp.dot(a_ref[...], w_ref[...], preferred_element_type=F32).astype(o_ref.dtype)
    if side:
        side.body(pl.program_id(0) * n_inner + pl.program_id(1), refs[:n_side_in], refs[n_side_in + 1:])


def matmul(a, w, out_dtype, n=None, side=None, bm_pref=1024, bn_pref=1024):
    m, k = a.shape
    n = w.shape[1] if n is None else n
    bm, bn = _tile(m, bm_pref), _tile(n, bn_pref)
    ni = m // bm
    if side and not side.plan((n // bn) * ni):
        side = None
    step = lambda j, i: j * ni + i
    side_in, side_out = side.specs(step) if side else ([], [])
    outs = pl.pallas_call(
        functools.partial(_mm_kernel, side=side, n_inner=ni),
        grid=(n // bn, ni),
        in_specs=[pl.BlockSpec((bm, k), lambda j, i: (i, 0)), pl.BlockSpec((k, bn), lambda j, i: (0, j))] + side_in,
        out_specs=[pl.BlockSpec((bm, bn), lambda j, i: (i, j))] + side_out,
        out_shape=[jax.ShapeDtypeStruct((m, n), out_dtype)] + (side.out_shapes if side else []),
        compiler_params=_params(("parallel", "arbitrary" if side else "parallel"), 60 if side else 56),
        name="matmul",
    )(a, w, *(side.arrays if side else ()))
    return outs[0], (tuple(outs[1:]) if side else None)


def _mm_res_kernel(a_ref, w_ref, x_ref, *refs, nk, side):
    n_side_in = len(side.arrays) if side else 0
    o_ref = refs[n_side_in]

    def partial_product():
        if side:
            step = (pl.program_id(0) * pl.num_programs(1) + pl.program_id(1)) * nk + pl.program_id(2)
            side.body(step, refs[:n_side_in], refs[n_side_in + 1:])
        return jnp.dot(a_ref[...], w_ref[...], preferred_element_type=F32)

    if nk == 1:
        o_ref[...] = x_ref[...] + partial_product()
        return
    kk = pl.program_id(2)

    @pl.when(kk == 0)
    def _():
        o_ref[...] = partial_product()

    @pl.when(jnp.logical_and(kk > 0, kk < nk - 1))
    def _():
        o_ref[...] += partial_product()

    @pl.when(kk == nk - 1)
    def _():
        o_ref[...] = x_ref[...] + (o_ref[...] + partial_product())


def matmul_residual(a, w, x, nk, side=None, bm_pref=1024, bn_pref=1024):
    m, k = a.shape
    n = w.shape[1]
    bm, bn, bk = _tile(m, bm_pref), _tile(n, bn_pref), k // nk
    assert bk * nk == k and (nk == 1 or bk % LANE == 0)
    nj = n // bn
    if side and not side.plan((m // bm) * nj * nk):
        side = None
    step = lambda i, j, kk: (i * nj + j) * nk + kk
    side_in, side_out = side.specs(step) if side else ([], [])
    outs = pl.pallas_call(
        functools.partial(_mm_res_kernel, nk=nk, side=side),
        grid=(m // bm, nj, nk),
        in_specs=[
            pl.BlockSpec((bm, bk), lambda i, j, kk: (i, kk)),
            pl.BlockSpec((bk, bn), lambda i, j, kk: (kk, j)),
            pl.BlockSpec((bm, bn), lambda i, j, kk: (i, j)),
        ] + side_in,
        out_specs=[pl.BlockSpec((bm, bn), lambda i, j, kk: (i, j))] + side_out,
        out_shape=[jax.ShapeDtypeStruct((m, n), F32)] + (side.out_shapes if side else []),
        compiler_params=_params(("arbitrary",) * 3 if side else ("parallel", "parallel", "arbitrary"), 56),
        name="matmul_residual",
    )(a, w, x, *(side.arrays if side else ()))
    return outs[0], (tuple(outs[1:]) if side else None)


def _as_column(row, length):
    return jnp.transpose(jnp.broadcast_to(row, (LANE, length)))[:, 0:1]


def _cumsum_lanes(row, tri):
    x = jnp.broadcast_to(row, (SUBLANE, row.shape[1]))
    hi = x.astype(BF16).astype(F32)
    r1 = x - hi
    mid = r1.astype(BF16).astype(F32)
    lo = (r1 - mid).astype(BF16).astype(F32)
    parts = jnp.concatenate([hi, mid, lo], axis=0).astype(BF16)
    acc = jnp.dot(parts, tri, preferred_element_type=F32)
    s = acc[0:SUBLANE] + acc[SUBLANE:2 * SUBLANE] + acc[2 * SUBLANE:3 * SUBLANE]
    return s[0:1, :]


def _mlstm_kernel(bi_ref, bf_ref, q_ref, k_ref, v_ref, o_ref, gf_ref, hg_ref, tri_ref, out_ref,
                  c_ref, n_ref, m_ref, *, chunk, dqk, dv, heads_per_step, qk_scale):
    @pl.when(pl.program_id(2) == 0)
    def _():
        c_ref[...] = jnp.zeros_like(c_ref)
        n_ref[...] = jnp.zeros_like(n_ref)
        m_ref[...] = jnp.zeros_like(m_ref)

    tri = tri_ref[...]
    t_idx = lax.broadcasted_iota(jnp.int32, (chunk, chunk), 0)
    s_idx = lax.broadcasted_iota(jnp.int32, (chunk, chunk), 1)
    causal = t_idx >= s_idx

    for hh in range(heads_per_step):
        h = pl.program_id(1) * heads_per_step + hh
        qs = slice(hh * dqk, (hh + 1) * dqk)
        vs = slice(hh * dv, (hh + 1) * dv)
        gf = gf_ref[hh]
        i_row = GATE_SOFTCAP * jnp.tanh((gf[0:1, :] + bi_ref[h]) / GATE_SOFTCAP)
        logf = jax.nn.log_sigmoid(GATE_SOFTCAP * jnp.tanh((gf[1:2, :] + bf_ref[h]) / GATE_SOFTCAP))
        b_row = _cumsum_lanes(logf, tri)
        d_row = i_row - b_row
        b_col = _as_column(b_row, chunk)
        d_col = _as_column(d_row, chunk)

        log_intra = jnp.where(causal, b_col + d_row, -jnp.inf)
        m_prev = m_ref[hh, 0:1, 0:1]
        log_inter = b_col + m_prev
        m_t = jnp.maximum(log_inter, jnp.max(log_intra, axis=1, keepdims=True))
        w_inter = jnp.exp(log_inter - m_t)
        w_intra = jnp.exp(log_intra - m_t)

        q = q_ref[:, qs] * qk_scale
        k = k_ref[:, qs]
        v = v_ref[:, vs]
        s = lax.dot_general(q, k, (((1,), (1,)), ((), ())), preferred_element_type=F32) * w_intra
        c_prev = c_ref[hh]
        n_prev = n_ref[hh]
        num = (w_inter * jnp.dot(q, c_prev.astype(BF16), preferred_element_type=F32)
               + jnp.dot(s.astype(BF16), v, preferred_element_type=F32))
        qn = jnp.sum(q.astype(F32) * n_prev, axis=1, keepdims=True)
        den = w_inter * qn + jnp.sum(s, axis=1, keepdims=True)
        hval = num / jnp.maximum(jnp.abs(den), jnp.exp(-m_t))
        hval = hval * lax.rsqrt(jnp.mean(hval * hval, axis=-1, keepdims=True) + RMS_EPS)
        hval = hval * hg_ref[:, vs]
        out_ref[:, vs] = (hval * jax.nn.sigmoid(o_ref[:, vs].astype(F32))).astype(out_ref.dtype)

        b_last = b_row[:, chunk - 1:chunk]
        log_prev = b_last + m_prev
        m_new = jnp.maximum(log_prev, jnp.max(b_last + d_row, axis=1, keepdims=True))
        a_prev = jnp.exp(log_prev - m_new)
        ka = jnp.exp(b_last + d_col - m_new) * k.astype(F32)
        c_ref[hh] = a_prev * c_prev + lax.dot_general(
            ka.astype(BF16), v, (((0,), (0,)), ((), ())), preferred_element_type=F32)
        n_ref[hh] = a_prev * n_prev + jnp.sum(ka, axis=0, keepdims=True)
        m_ref[hh] = jnp.broadcast_to(m_new, m_ref.shape[1:])


def mlstm_recurrence(proj, gates, b_i, b_f, head_g, batch, seq):
    heads = b_i.shape[0]
    dv = head_g.shape[0] // heads
    dqk = dv // 2
    chunk = _tile(seq, MLSTM_CHUNK)
    nc = seq // chunk
    hps = _tile(heads, MLSTM_HEADS_PER_STEP)
    proj3 = proj.reshape(batch, seq, proj.shape[1])
    gf = gates[:, :2 * heads].reshape(batch, nc, chunk, 2, heads).transpose(0, 4, 1, 3, 2)
    tri = (jnp.arange(chunk)[:, None] <= jnp.arange(chunk)[None, :]).astype(BF16)
    k_off = heads // hps
    v_off = 2 * heads * dqk // (dv * hps)
    o_off = v_off + heads // hps
    smem = pl.BlockSpec(memory_space=pltpu.SMEM)
    out = pl.pallas_call(
        functools.partial(_mlstm_kernel, chunk=chunk, dqk=dqk, dv=dv, heads_per_step=hps,
                          qk_scale=float(dqk) ** -0.5),
        grid=(batch, heads // hps, nc),
        in_specs=[
            smem, smem,
            pl.BlockSpec((None, chunk, hps * dqk), lambda b, h, c: (b, c, h)),
            pl.BlockSpec((None, chunk, hps * dqk), lambda b, h, c: (b, c, k_off + h)),
            pl.BlockSpec((None, chunk, hps * dv), lambda b, h, c: (b, c, v_off + h)),
            pl.BlockSpec((None, chunk, hps * dv), lambda b, h, c: (b, c, o_off + h)),
            pl.BlockSpec((None, hps, None, 2, chunk), lambda b, h, c: (b, h, c, 0, 0)),
            pl.BlockSpec((1, hps * dv), lambda b, h, c: (0, h)),
            pl.BlockSpec((chunk, chunk), lambda b, h, c: (0, 0)),
        ],
        out_specs=pl.BlockSpec((None, chunk, hps * dv), lambda b, h, c: (b, c, h)),
        out_shape=jax.ShapeDtypeStruct((batch, seq, heads * dv), BF16),
        scratch_shapes=[
            pltpu.VMEM((hps, dqk, dv), F32),
            pltpu.VMEM((hps, 1, dqk), F32),
            pltpu.VMEM((hps, SUBLANE, LANE), F32),
        ],
        compiler_params=_params(("parallel", "parallel", "arbitrary"), 48),
        name="mlstm_recurrence",
    )(b_i, b_f, proj3, proj3, proj3, proj3, gf, head_g.reshape(1, heads * dv), tri)
    return out.reshape(batch * seq, heads * dv)


def _ffn_in_kernel(a_ref, wg_ref, wu_ref, cwg_ref, cwu_ref, cbg_ref, cbu_ref, o_ref, hg_ref, hu_ref,
                   *, bm, tiles_per_seq, last_width):
    first = (pl.program_id(1) % tiles_per_seq) == 0

    @pl.when(first)
    def _():
        hg_ref[0:SUBLANE, :] = jnp.zeros((SUBLANE, hg_ref.shape[1]), F32)
        hu_ref[0:SUBLANE, :] = jnp.zeros((SUBLANE, hu_ref.shape[1]), F32)

    @pl.when(jnp.logical_not(first))
    def _():
        hg_ref[0:SUBLANE, :] = hg_ref[bm:bm + SUBLANE, :]
        hu_ref[0:SUBLANE, :] = hu_ref[bm:bm + SUBLANE, :]

    def tile(width):
        cs = slice(0, width)
        a = a_ref[...]

        def conv_half(w_ref, cw_ref, cb_ref, h_ref):
            hcur = jnp.dot(a, w_ref[:, cs], preferred_element_type=F32)
            h_ref[SUBLANE:SUBLANE + bm, cs] = hcur
            hall = h_ref[0:SUBLANE + bm, cs]
            h1 = pltpu.roll(hall, 1, axis=0)[SUBLANE:, :]
            h2 = pltpu.roll(hall, 2, axis=0)[SUBLANE:, :]
            cw = cw_ref[:, cs]
            return cb_ref[:, cs] + cw[0:1, :] * h2 + cw[1:2, :] * h1 + cw[2:3, :] * hcur

        g = conv_half(wg_ref, cwg_ref, cbg_ref, hg_ref)
        u = conv_half(wu_ref, cwu_ref, cbu_ref, hu_ref)
        o_ref[:, cs] = (jax.nn.silu(g) * u).astype(o_ref.dtype)

    bn = o_ref.shape[1]
    if last_width == bn:
        tile(bn)
    else:
        last = pl.program_id(0) == pl.num_programs(0) - 1

        @pl.when(jnp.logical_not(last))
        def _():
            tile(bn)

        @pl.when(last)
        def _():
            tile(last_width)
            o_ref[:, last_width:] = jnp.zeros((bm, bn - last_width), o_ref.dtype)


def ffn_in(a, w_in, conv_w, conv_b, width, seq, bm_pref=1024):
    t, d = a.shape
    half = w_in.shape[1] // 2
    bm, bn = _tile(seq, bm_pref), min(FFN_IN_COLS, half)
    nj = half // bn
    assert nj * bn == half and (nj - 1) * bn < width <= half and width % LANE == 0
    return pl.pallas_call(
        functools.partial(_ffn_in_kernel, bm=bm, tiles_per_seq=seq // bm, last_width=width - (nj - 1) * bn),
        grid=(nj, t // bm),
        in_specs=[
            pl.BlockSpec((bm, d), lambda j, i: (i, 0)),
            pl.BlockSpec((d, bn), lambda j, i: (0, j)),
            pl.BlockSpec((d, bn), lambda j, i: (0, nj + j)),
            pl.BlockSpec((FFN_CONV_WIDTH, bn), lambda j, i: (0, j)),
            pl.BlockSpec((FFN_CONV_WIDTH, bn), lambda j, i: (0, nj + j)),
            pl.BlockSpec((1, bn), lambda j, i: (0, j)),
            pl.BlockSpec((1, bn), lambda j, i: (0, nj + j)),
        ],
        out_specs=pl.BlockSpec((bm, bn), lambda j, i: (i, j)),
        out_shape=jax.ShapeDtypeStruct((t, half), BF16),
        scratch_shapes=[pltpu.VMEM((bm + 2 * SUBLANE, bn), F32), pltpu.VMEM((bm + 2 * SUBLANE, bn), F32)],
        compiler_params=_params(("parallel", "arbitrary"), 56),
        name="ffn_in",
    )(a, w_in, w_in, conv_w, conv_w, conv_b, conv_b)


def _pool_kernel(x_ref, g_ref, w_ref, sc_ref, gn_ref, o_ref, on_ref, e_ref, p_ref, *, bm, tiles_per_seq, group_dim):
    step = pl.program_id(0)
    tile_in_seq = step % tiles_per_seq
    halo = POOL_HALO
    rows = halo + bm

    @pl.when(step == 0)
    def _():
        e_ref[0:halo, :] = jnp.zeros((halo, e_ref.shape[1]), F32)
        p_ref[0:SUBLANE, :] = jnp.zeros((SUBLANE, p_ref.shape[1]), F32)

    @pl.when(jnp.logical_and(step > 0, tile_in_seq == 0))
    def _():
        e_ref[0:halo, :] = jnp.zeros((halo, e_ref.shape[1]), F32)

    @pl.when(tile_in_seq > 0)
    def _():
        e_ref[0:halo, :] = e_ref[bm:bm + halo, :]

    x = x_ref[...]
    xn = x * lax.rsqrt(jnp.mean(x * x, axis=-1, keepdims=True) + RMS_EPS) * g_ref[...]
    e_ref[halo:rows, :] = xn

    pos = tile_in_seq * bm + lax.broadcasted_iota(jnp.int32, (bm, 1), 0)
    lo = SUBLANE
    for gi, window in enumerate(POOL_WINDOWS):
        cs = slice(gi * group_dim, (gi + 1) * group_dim)
        src, dst, span = e_ref, p_ref, 1
        while span < window:
            dst[lo:rows, cs] = src[lo:rows, cs] + src[lo - span:rows - span, cs]
            src, dst = dst, (e_ref if dst is p_ref else p_ref)
            span *= 2
        wsum = src[halo:rows, cs]
        cnt = jnp.minimum(pos + 1, window).astype(F32)
        pooled = wsum / cnt - xn[:, cs]
        y = jnp.dot(pooled.astype(BF16), w_ref[gi], preferred_element_type=F32)
        o_ref[:, cs] = x[:, cs] + y * sc_ref[:, cs]
        if window > 2:
            e_ref[halo:rows, cs] = xn[:, cs]

    y = o_ref[...]
    on_ref[...] = (y * lax.rsqrt(jnp.mean(y * y, axis=-1, keepdims=True) + RMS_EPS) * gn_ref[...]).astype(on_ref.dtype)


def pool_mixer(x2d, norm_g, w_group, scale, next_norm_g, seq, bm_pref=256):
    t, d = x2d.shape
    groups, group_dim, _ = w_group.shape
    bm = _tile(seq, bm_pref)
    return pl.pallas_call(
        functools.partial(_pool_kernel, bm=bm, tiles_per_seq=seq // bm, group_dim=group_dim),
        grid=(t // bm,),
        in_specs=[
            pl.BlockSpec((bm, d), lambda i: (i, 0)),
            pl.BlockSpec((1, d), lambda i: (0, 0)),
            pl.BlockSpec((groups, group_dim, group_dim), lambda i: (0, 0, 0)),
            pl.BlockSpec((1, d), lambda i: (0, 0)),
            pl.BlockSpec((1, d), lambda i: (0, 0)),
        ],
        out_specs=[pl.BlockSpec((bm, d), lambda i: (i, 0)), pl.BlockSpec((bm, d), lambda i: (i, 0))],
        out_shape=[jax.ShapeDtypeStruct((t, d), F32), jax.ShapeDtypeStruct((t, d), BF16)],
        scratch_shapes=[pltpu.VMEM((POOL_HALO + bm, d), F32), pltpu.VMEM((POOL_HALO + bm, d), F32)],
        compiler_params=_params(("arbitrary",), 56),
        name="pool_mixer",
    )(x2d, norm_g.reshape(1, d), w_group, scale.reshape(1, d), next_norm_g.reshape(1, d))


def _prep_w_in_kernel(w_ref, o_ref, *, d_ff, fp):
    rows = w_ref.shape[0]
    full = (d_ff // LANE) * LANE
    tail = d_ff - full
    x = w_ref[...]
    o_ref[:, 0:full] = x[:, 0:full].astype(BF16)
    if tail == 0:
        up = x[:, d_ff:2 * d_ff]
    else:
        keep = lax.broadcasted_iota(jnp.int32, (rows, LANE), 1) < tail
        o_ref[:, full:full + LANE] = jnp.where(keep, x[:, full:full + LANE], 0.0).astype(BF16)
        src = x[:, full:2 * d_ff]
        up = pltpu.roll(src, src.shape[1] - tail, axis=1)
        o_ref[:, fp + full:fp + full + LANE] = jnp.where(keep, up[:, full:full + LANE], 0.0).astype(BF16)
    o_ref[:, fp:fp + full] = up[:, 0:full].astype(BF16)
    used = full + (LANE if tail else 0)
    if used < fp:
        zeros = jnp.zeros((rows, fp - used), BF16)
        o_ref[:, used:fp] = zeros
        o_ref[:, fp + used:2 * fp] = zeros


def prep_w_in(w_in_all, layer, fp):
    _, d, two_ff = w_in_all.shape
    d_ff = two_ff // 2
    rows = _tile(d, 128)
    return pl.pallas_call(
        functools.partial(_prep_w_in_kernel, d_ff=d_ff, fp=fp),
        grid=(d // rows,),
        in_specs=[pl.BlockSpec((None, rows, two_ff), lambda i: (layer, i, 0))],
        out_specs=pl.BlockSpec((rows, 2 * fp), lambda i: (i, 0)),
        out_shape=jax.ShapeDtypeStruct((d, 2 * fp), BF16),
        compiler_params=_params(("parallel",), 48),
        name="prep_w_in",
    )(w_in_all)


def _prep_w_out_kernel(w_ref, o_ref, *, n_src):
    @pl.when(pl.program_id(0) < n_src)
    def _():
        o_ref[...] = w_ref[...].astype(BF16)

    @pl.when(pl.program_id(0) >= n_src)
    def _():
        o_ref[...] = jnp.zeros_like(o_ref)


def prep_w_out(w_out_all, layer, fp):
    _, d_ff, d = w_out_all.shape
    unit = 64
    assert d_ff % unit == 0
    units = d_ff // unit
    rows = unit * max(c for c in range(1, units + 1)
                      if units % c == 0 and c * unit * d * 4 <= 10 * MIB)
    n_src = d_ff // rows
    n_blocks = -(-fp // rows)
    return pl.pallas_call(
        functools.partial(_prep_w_out_kernel, n_src=n_src),
        grid=(n_blocks,),
        in_specs=[pl.BlockSpec((None, rows, d), lambda i: (layer, jnp.minimum(i, n_src - 1), 0))],
        out_specs=pl.BlockSpec((rows, d), lambda i: (i, 0)),
        out_shape=jax.ShapeDtypeStruct((n_blocks * rows, d), BF16),
        compiler_params=_params(("parallel",), 40),
        name="prep_w_out",
    )(w_out_all)


class FfnWeightPrep:
    MAX_BLOCK_BYTES = 4 * MIB

    def __init__(self, w_in_all, w_out_all, layer, fp):
        _, d, two_ff = w_in_all.shape
        self.d, self.d_ff, self.fp, self.layer = d, two_ff // 2, fp, layer
        self.arrays = (w_in_all, w_out_all)
        self.out_shapes = [jax.ShapeDtypeStruct((d, 2 * fp), BF16), jax.ShapeDtypeStruct((fp, d), BF16)]

    @staticmethod
    def _rows(total, unit, max_blocks):
        for c in range(1, total // unit + 1):
            if total % (c * unit) == 0 and total // (c * unit) <= max_blocks:
                return c * unit
        return None

    def plan(self, n_steps):
        unit_out = 16
        while self.d_ff % (unit_out * 2) == 0 and self.fp % (unit_out * 2) == 0:
            unit_out *= 2
        self.in_rows = self._rows(self.d, 16, n_steps)
        self.out_rows = self._rows(self.fp, unit_out, n_steps)
        if self.in_rows is None or self.out_rows is None or self.d_ff % self.out_rows:
            return False
        self.n_in, self.n_src, self.n_out = self.d // self.in_rows, self.d_ff // self.out_rows, self.fp // self.out_rows
        return max(self.in_rows * 2 * self.d_ff, self.out_rows * self.d) * 4 <= self.MAX_BLOCK_BYTES

    def specs(self, step):
        layer, n_in, n_src, n_out = self.layer, self.n_in, self.n_src, self.n_out
        ins = [pl.BlockSpec((None, self.in_rows, 2 * self.d_ff),
                            lambda *g: (layer, jnp.minimum(step(*g), n_in - 1), 0)),
               pl.BlockSpec((None, self.out_rows, self.d),
                            lambda *g: (layer, jnp.minimum(step(*g), n_src - 1), 0))]
        outs = [pl.BlockSpec((self.in_rows, 2 * self.fp), lambda *g: (jnp.minimum(step(*g), n_in - 1), 0)),
                pl.BlockSpec((self.out_rows, self.d), lambda *g: (jnp.minimum(step(*g), n_out - 1), 0))]
        return ins, outs

    def body(self, step, in_refs, out_refs):
        _prep_w_in_kernel(in_refs[0], out_refs[0], d_ff=self.d_ff, fp=self.fp)
        real = jnp.minimum(step, self.n_out - 1) < self.n_src
        out_refs[1][...] = jnp.where(real, in_refs[1][...], 0.0).astype(BF16)


def _pad_cols(w, width):
    return jnp.pad(w, [(0, 0)] * (w.ndim - 1) + [(0, width - w.shape[-1])])


FFN_OUT_K_STEPS = 4
MXU_TILE = 256


def _padded_hidden(d_ff):
    width = -(-d_ff // MXU_TILE) * MXU_TILE
    cols = min(FFN_IN_COLS, width)
    stride = -(-width // cols) * cols
    assert stride % (LANE * FFN_OUT_K_STEPS) == 0
    return width, stride


def _conv_ffn(x2d, xn, w_in_all, conv_w, conv_b, w_out_all, layer, seq, prepared=None, side=None):
    d_ff = w_out_all.shape[1]
    width, stride = _padded_hidden(d_ff)
    w_in_p, w_out_p = prepared or (prep_w_in(w_in_all, layer, stride), prep_w_out(w_out_all, layer, stride))
    conv_w_p = _pad_cols(conv_w.reshape(FFN_CONV_WIDTH, 2, d_ff), stride).reshape(FFN_CONV_WIDTH, 2 * stride)
    conv_b_p = _pad_cols(conv_b.reshape(1, 2, d_ff), stride).reshape(1, 2 * stride)
    act = ffn_in(xn, w_in_p, conv_w_p, conv_b_p, width, seq)
    return matmul_residual(act, w_out_p, x2d, FFN_OUT_K_STEPS, side=side)


def _mlstm_layer(x2d, norm_g, w_in, b_i, b_f, head_g, w_out, batch, seq, side=None):
    heads = b_i.shape[0]
    main = w_in.shape[1] - 2 * heads
    xn = rmsnorm(x2d, norm_g, BF16)
    proj, side_out = matmul(xn, w_in.astype(BF16), BF16, n=main, side=side)
    gates, _ = matmul(xn, _pad_cols(w_in[:, main:], LANE).astype(BF16), F32)
    hgated = mlstm_recurrence(proj, gates, b_i, b_f, head_g, batch, seq)
    out, _ = matmul_residual(hgated, w_out.astype(BF16), x2d, 1)
    return out, side_out


def kernel(x, mlstm_norm_g, mlstm_w_in, mlstm_b_i, mlstm_b_f, mlstm_head_g, mlstm_w_out,
           pool_norm_g, pool_w_group, pool_scale,
           ffn_norm_g, ffn_w_in, ffn_conv_w, ffn_conv_b, ffn_w_out, final_norm_g):
    batch, seq, d = x.shape
    depth = ffn_w_in.shape[0]
    _, fp = _padded_hidden(ffn_w_out.shape[1])
    x2d = x.reshape(batch * seq, d)
    prepared = {}
    for layer in range(depth):
        j = layer // 2
        if layer % 2 == 0:
            x2d, prepared[layer] = _mlstm_layer(x2d, mlstm_norm_g[j], mlstm_w_in[j], mlstm_b_i[j], mlstm_b_f[j],
                                                mlstm_head_g[j], mlstm_w_out[j], batch, seq,
                                                side=FfnWeightPrep(ffn_w_in, ffn_w_out, layer, fp))
            xn = rmsnorm(x2d, ffn_norm_g[layer], BF16)
            side = FfnWeightPrep(ffn_w_in, ffn_w_out, layer + 1, fp) if layer + 1 < depth else None
        else:
            x2d, xn = pool_mixer(x2d, pool_norm_g[j], pool_w_group[j].astype(BF16), pool_scale[j],
                                 ffn_norm_g[layer], seq)
            side = None
        x2d, prepared[layer + 1] = _conv_ffn(x2d, xn, ffn_w_in, ffn_conv_w[layer], ffn_conv_b[layer], ffn_w_out,
                                             layer, seq, prepared=prepared.get(layer), side=side)
    return rmsnorm(x2d, final_norm_g, x.dtype).reshape(batch, seq, d)
```

```python
import functools

import jax
import jax.numpy as jnp
from jax import lax
from jax.experimental import pallas as pl
from jax.experimental.pallas import tpu as pltpu

RMS_EPS = 1e-6
GATE_SOFTCAP = 15.0
POOL_WINDOWS = (2, 4, 8, 16)
FFN_CONV_WIDTH = 3
MLSTM_CHUNK = 256
MLSTM_HEADS_PER_STEP = 2
FFN_IN_COLS = 512
LANE = 128
SUBLANE = 8
POOL_HALO = 32
MIB = 1024 * 1024

F32 = jnp.float32
BF16 = jnp.bfloat16


def _params(semantics, vmem_mib):
    return pltpu.CompilerParams(dimension_semantics=semantics, vmem_limit_bytes=vmem_mib * MIB)


def _tile(dim, pref):
    t = min(dim, pref)
    while dim % t:
        t //= 2
    return t


def _rmsnorm_kernel(x_ref, g_ref, o_ref):
    x = x_ref[...]
    y = x * lax.rsqrt(jnp.mean(x * x, axis=-1, keepdims=True) + RMS_EPS)
    o_ref[...] = (y * g_ref[...]).astype(o_ref.dtype)


def rmsnorm(x2d, g, out_dtype):
    t, d = x2d.shape
    bm = _tile(t, 512)
    return pl.pallas_call(
        _rmsnorm_kernel,
        grid=(t // bm,),
        in_specs=[pl.BlockSpec((bm, d), lambda i: (i, 0)), pl.BlockSpec((1, d), lambda i: (0, 0))],
        out_specs=pl.BlockSpec((bm, d), lambda i: (i, 0)),
        out_shape=jax.ShapeDtypeStruct((t, d), out_dtype),
        compiler_params=_params(("parallel",), 48),
        name="rmsnorm",
    )(x2d, g.reshape(1, d))


def _mm_kernel(a_ref, w_ref, *refs, side, n_inner):
    n_side_in = len(side.arrays) if side else 0
    o_ref = refs[n_side_in]
    o_ref[...] = jnp.dot(a_ref[...], w_ref[...], preferred_element_type=F32).astype(o_ref.dtype)
    if side:
        side.body(pl.program_id(0) * n_inner + pl.program_id(1), refs[:n_side_in], refs[n_side_in + 1:])


def matmul(a, w, out_dtype, n=None, side=None, bm_pref=1024, bn_pref=1024):
    m, k = a.shape
    n = w.shape[1] if n is None else n
    bm, bn = _tile(m, bm_pref), _tile(n, bn_pref)
    ni = m // bm
    if side and not side.plan((n // bn) * ni):
        side = None
    step = lambda j, i: j * ni + i
    side_in, side_out = side.specs(step) if side else ([], [])
    outs = pl.pallas_call(
        functools.partial(_mm_kernel, side=side, n_inner=ni),
        grid=(n // bn, ni),
        in_specs=[pl.BlockSpec((bm, k), lambda j, i: (i, 0)), pl.BlockSpec((k, bn), lambda j, i: (0, j))] + side_in,
        out_specs=[pl.BlockSpec((bm, bn), lambda j, i: (i, j))] + side_out,
        out_shape=[jax.ShapeDtypeStruct((m, n), out_dtype)] + (side.out_shapes if side else []),
        compiler_params=_params(("parallel", "arbitrary" if side else "parallel"), 60 if side else 56),
        name="matmul",
    )(a, w, *(side.arrays if side else ()))
    return outs[0], (tuple(outs[1:]) if side else None)


def _mm_res_kernel(a_ref, w_ref, x_ref, *refs, nk, last_bk, side):
    n_side_in = len(side.arrays) if side else 0
    o_ref = refs[n_side_in]

    def partial_product(bk=None):
        if side:
            step = (pl.program_id(0) * pl.num_programs(1) + pl.program_id(1)) * nk + pl.program_id(2)
            side.body(step, refs[:n_side_in], refs[n_side_in + 1:])
        return jnp.dot(a_ref[:, :bk], w_ref[:bk, :], preferred_element_type=F32)

    if nk == 1:
        o_ref[...] = x_ref[...] + partial_product(last_bk)
        return
    kk = pl.program_id(2)

    @pl.when(kk == 0)
    def _():
        o_ref[...] = partial_product()

    @pl.when(jnp.logical_and(kk > 0, kk < nk - 1))
    def _():
        o_ref[...] += partial_product()

    @pl.when(kk == nk - 1)
    def _():
        o_ref[...] = x_ref[...] + (o_ref[...] + partial_product(last_bk))


def matmul_residual(a, w, x, nk, k_real=None, side=None, bm_pref=1024, bn_pref=1024):
    m, k = a.shape
    n = w.shape[1]
    bm, bn, bk = _tile(m, bm_pref), _tile(n, bn_pref), k // nk
    assert bk * nk == k and (nk == 1 or bk % LANE == 0)
    last_bk = bk - (k - (k if k_real is None else k_real))
    assert 0 < last_bk <= bk and last_bk % LANE == 0
    nj = n // bn
    if side and not side.plan((m // bm) * nj * nk):
        side = None
    step = lambda i, j, kk: (i * nj + j) * nk + kk
    side_in, side_out = side.specs(step) if side else ([], [])
    outs = pl.pallas_call(
        functools.partial(_mm_res_kernel, nk=nk, last_bk=last_bk, side=side),
        grid=(m // bm, nj, nk),
        in_specs=[
            pl.BlockSpec((bm, bk), lambda i, j, kk: (i, kk)),
            pl.BlockSpec((bk, bn), lambda i, j, kk: (kk, j)),
            pl.BlockSpec((bm, bn), lambda i, j, kk: (i, j)),
        ] + side_in,
        out_specs=[pl.BlockSpec((bm, bn), lambda i, j, kk: (i, j))] + side_out,
        out_shape=[jax.ShapeDtypeStruct((m, n), F32)] + (side.out_shapes if side else []),
        compiler_params=_params(("arbitrary",) * 3 if side else ("parallel", "parallel", "arbitrary"), 56),
        name="matmul_residual",
    )(a, w, x, *(side.arrays if side else ()))
    return outs[0], (tuple(outs[1:]) if side else None)


def _as_column(row, length):
    return jnp.transpose(jnp.broadcast_to(row, (LANE, length)))[:, 0:1]


def _cumsum_lanes(row, tri):
    x = jnp.broadcast_to(row, (SUBLANE, row.shape[1]))
    hi = x.astype(BF16).astype(F32)
    r1 = x - hi
    mid = r1.astype(BF16).astype(F32)
    lo = (r1 - mid).astype(BF16).astype(F32)
    parts = jnp.concatenate([hi, mid, lo], axis=0).astype(BF16)
    acc = jnp.dot(parts, tri, preferred_element_type=F32)
    s = acc[0:SUBLANE] + acc[SUBLANE:2 * SUBLANE] + acc[2 * SUBLANE:3 * SUBLANE]
    return s[0:1, :]


def _mlstm_kernel(bi_ref, bf_ref, q_ref, k_ref, v_ref, o_ref, gf_ref, hg_ref, tri_ref, out_ref,
                  c_ref, n_ref, m_ref, *, chunk, dqk, dv, heads_per_step, qk_scale):
    @pl.when(pl.program_id(2) == 0)
    def _():
        c_ref[...] = jnp.zeros_like(c_ref)
        n_ref[...] = jnp.zeros_like(n_ref)
        m_ref[...] = jnp.zeros_like(m_ref)

    tri = tri_ref[...]
    t_idx = lax.broadcasted_iota(jnp.int32, (chunk, chunk), 0)
    s_idx = lax.broadcasted_iota(jnp.int32, (chunk, chunk), 1)
    causal = t_idx >= s_idx

    for hh in range(heads_per_step):
        h = pl.program_id(1) * heads_per_step + hh
        qs = slice(hh * dqk, (hh + 1) * dqk)
        vs = slice(hh * dv, (hh + 1) * dv)
        gf = gf_ref[hh]
        i_row = GATE_SOFTCAP * jnp.tanh((gf[0:1, :] + bi_ref[h]) / GATE_SOFTCAP)
        logf = jax.nn.log_sigmoid(GATE_SOFTCAP * jnp.tanh((gf[1:2, :] + bf_ref[h]) / GATE_SOFTCAP))
        b_row = _cumsum_lanes(logf, tri)
        d_row = i_row - b_row
        b_col = _as_column(b_row, chunk)
        d_col = _as_column(d_row, chunk)

        log_intra = jnp.where(causal, b_col + d_row, -jnp.inf)
        m_prev = m_ref[hh, 0:1, 0:1]
        log_inter = b_col + m_prev
        m_t = jnp.maximum(log_inter, jnp.max(log_intra, axis=1, keepdims=True))
        w_inter = jnp.exp(log_inter - m_t)
        w_intra = jnp.exp(log_intra - m_t)

        q = q_ref[:, qs] * qk_scale
        k = k_ref[:, qs]
        v = v_ref[:, vs]
        s = lax.dot_general(q, k, (((1,), (1,)), ((), ())), preferred_element_type=F32) * w_intra
        c_prev = c_ref[hh]
        n_prev = n_ref[hh]
        num = (w_inter * jnp.dot(q, c_prev.astype(BF16), preferred_element_type=F32)
               + jnp.dot(s.astype(BF16), v, preferred_element_type=F32))
        qn = jnp.sum(q.astype(F32) * n_prev, axis=1, keepdims=True)
        den = w_inter * qn + jnp.sum(s, axis=1, keepdims=True)
        hval = num / jnp.maximum(jnp.abs(den), jnp.exp(-m_t))
        hval = hval * lax.rsqrt(jnp.mean(hval * hval, axis=-1, keepdims=True) + RMS_EPS)
        hval = hval * hg_ref[:, vs]
        out_ref[:, vs] = (hval * jax.nn.sigmoid(o_ref[:, vs].astype(F32))).astype(out_ref.dtype)

        b_last = b_row[:, chunk - 1:chunk]
        log_prev = b_last + m_prev
        m_new = jnp.maximum(log_prev, jnp.max(b_last + d_row, axis=1, keepdims=True))
        a_prev = jnp.exp(log_prev - m_new)
        ka = jnp.exp(b_last + d_col - m_new) * k.astype(F32)
        c_ref[hh] = a_prev * c_prev + lax.dot_general(
            ka.astype(BF16), v, (((0,), (0,)), ((), ())), preferred_element_type=F32)
        n_ref[hh] = a_prev * n_prev + jnp.sum(ka, axis=0, keepdims=True)
        m_ref[hh] = jnp.broadcast_to(m_new, m_ref.shape[1:])


def mlstm_recurrence(proj, gates, b_i, b_f, head_g, batch, seq):
    heads = b_i.shape[0]
    dv = head_g.shape[0] // heads
    dqk = dv // 2
    chunk = _tile(seq, MLSTM_CHUNK)
    nc = seq // chunk
    hps = _tile(heads, MLSTM_HEADS_PER_STEP)
    proj3 = proj.reshape(batch, seq, proj.shape[1])
    gf = gates[:, :2 * heads].reshape(batch, nc, chunk, 2, heads).transpose(0, 4, 1, 3, 2)
    tri = (jnp.arange(chunk)[:, None] <= jnp.arange(chunk)[None, :]).astype(BF16)
    k_off = heads // hps
    v_off = 2 * heads * dqk // (dv * hps)
    o_off = v_off + heads // hps
    smem = pl.BlockSpec(memory_space=pltpu.SMEM)
    out = pl.pallas_call(
        functools.partial(_mlstm_kernel, chunk=chunk, dqk=dqk, dv=dv, heads_per_step=hps,
                          qk_scale=float(dqk) ** -0.5),
        grid=(batch, heads // hps, nc),
        in_specs=[
            smem, smem,
            pl.BlockSpec((None, chunk, hps * dqk), lambda b, h, c: (b, c, h)),
            pl.BlockSpec((None, chunk, hps * dqk), lambda b, h, c: (b, c, k_off + h)),
            pl.BlockSpec((None, chunk, hps * dv), lambda b, h, c: (b, c, v_off + h)),
            pl.BlockSpec((None, chunk, hps * dv), lambda b, h, c: (b, c, o_off + h)),
            pl.BlockSpec((None, hps, None, 2, chunk), lambda b, h, c: (b, h, c, 0, 0)),
            pl.BlockSpec((1, hps * dv), lambda b, h, c: (0, h)),
            pl.BlockSpec((chunk, chunk), lambda b, h, c: (0, 0)),
        ],
        out_specs=pl.BlockSpec((None, chunk, hps * dv), lambda b, h, c: (b, c, h)),
        out_shape=jax.ShapeDtypeStruct((batch, seq, heads * dv), BF16),
        scratch_shapes=[
            pltpu.VMEM((hps, dqk, dv), F32),
            pltpu.VMEM((hps, 1, dqk), F32),
            pltpu.VMEM((hps, SUBLANE, LANE), F32),
        ],
        compiler_params=_params(("parallel", "parallel", "arbitrary"), 48),
        name="mlstm_recurrence",
    )(b_i, b_f, proj3, proj3, proj3, proj3, gf, head_g.reshape(1, heads * dv), tri)
    return out.reshape(batch * seq, heads * dv)


def _ffn_in_kernel(a_ref, wg_ref, wu_ref, cwg_ref, cwu_ref, cbg_ref, cbu_ref, o_ref, hg_ref, hu_ref,
                   *, bm, tiles_per_seq, last_width):
    first = (pl.program_id(1) % tiles_per_seq) == 0

    @pl.when(first)
    def _():
        hg_ref[0:SUBLANE, :] = jnp.zeros((SUBLANE, hg_ref.shape[1]), F32)
        hu_ref[0:SUBLANE, :] = jnp.zeros((SUBLANE, hu_ref.shape[1]), F32)

    @pl.when(jnp.logical_not(first))
    def _():
        hg_ref[0:SUBLANE, :] = hg_ref[bm:bm + SUBLANE, :]
        hu_ref[0:SUBLANE, :] = hu_ref[bm:bm + SUBLANE, :]

    def tile(width):
        cs = slice(0, width)
        a = a_ref[...]

        def conv_half(w_ref, cw_ref, cb_ref, h_ref):
            hcur = jnp.dot(a, w_ref[:, cs], preferred_element_type=F32)
            h_ref[SUBLANE:SUBLANE + bm, cs] = hcur
            hall = h_ref[0:SUBLANE + bm, cs]
            h1 = pltpu.roll(hall, 1, axis=0)[SUBLANE:, :]
            h2 = pltpu.roll(hall, 2, axis=0)[SUBLANE:, :]
            cw = cw_ref[:, cs]
            return cb_ref[:, cs] + cw[0:1, :] * h2 + cw[1:2, :] * h1 + cw[2:3, :] * hcur

        g = conv_half(wg_ref, cwg_ref, cbg_ref, hg_ref)
        u = conv_half(wu_ref, cwu_ref, cbu_ref, hu_ref)
        o_ref[:, cs] = (jax.nn.silu(g) * u).astype(o_ref.dtype)

    bn = o_ref.shape[1]
    if last_width == bn:
        tile(bn)
    else:
        last = pl.program_id(0) == pl.num_programs(0) - 1

        @pl.when(jnp.logical_not(last))
        def _():
            tile(bn)

        @pl.when(last)
        def _():
            tile(last_width)
            o_ref[:, last_width:] = jnp.zeros((bm, bn - last_width), o_ref.dtype)


def ffn_in(a, w_in, conv_w, conv_b, width, seq, bm_pref=1024):
    t, d = a.shape
    half = w_in.shape[1] // 2
    bm, bn = _tile(seq, bm_pref), min(FFN_IN_COLS, half)
    nj = half // bn
    assert nj * bn == half and (nj - 1) * bn < width <= half and width % LANE == 0
    return pl.pallas_call(
        functools.partial(_ffn_in_kernel, bm=bm, tiles_per_seq=seq // bm, last_width=width - (nj - 1) * bn),
        grid=(nj, t // bm),
        in_specs=[
            pl.BlockSpec((bm, d), lambda j, i: (i, 0)),
            pl.BlockSpec((d, bn), lambda j, i: (0, j)),
            pl.BlockSpec((d, bn), lambda j, i: (0, nj + j)),
            pl.BlockSpec((FFN_CONV_WIDTH, bn), lambda j, i: (0, j)),
            pl.BlockSpec((FFN_CONV_WIDTH, bn), lambda j, i: (0, nj + j)),
            pl.BlockSpec((1, bn), lambda j, i: (0, j)),
            pl.BlockSpec((1, bn), lambda j, i: (0, nj + j)),
        ],
        out_specs=pl.BlockSpec((bm, bn), lambda j, i: (i, j)),
        out_shape=jax.ShapeDtypeStruct((t, half), BF16),
        scratch_shapes=[pltpu.VMEM((bm + 2 * SUBLANE, bn), F32), pltpu.VMEM((bm + 2 * SUBLANE, bn), F32)],
        compiler_params=_params(("parallel", "arbitrary"), 56),
        name="ffn_in",
    )(a, w_in, w_in, conv_w, conv_w, conv_b, conv_b)


def _pool_kernel(x_ref, g_ref, w_ref, sc_ref, gn_ref, o_ref, on_ref, e_ref, p_ref, *, bm, tiles_per_seq, group_dim):
    step = pl.program_id(0)
    tile_in_seq = step % tiles_per_seq
    halo = POOL_HALO
    rows = halo + bm

    @pl.when(step == 0)
    def _():
        e_ref[0:halo, :] = jnp.zeros((halo, e_ref.shape[1]), F32)
        p_ref[0:SUBLANE, :] = jnp.zeros((SUBLANE, p_ref.shape[1]), F32)

    @pl.when(jnp.logical_and(step > 0, tile_in_seq == 0))
    def _():
        e_ref[0:halo, :] = jnp.zeros((halo, e_ref.shape[1]), F32)

    @pl.when(tile_in_seq > 0)
    def _():
        e_ref[0:halo, :] = e_ref[bm:bm + halo, :]

    x = x_ref[...]
    xn = x * lax.rsqrt(jnp.mean(x * x, axis=-1, keepdims=True) + RMS_EPS) * g_ref[...]
    e_ref[halo:rows, :] = xn

    pos = tile_in_seq * bm + lax.broadcasted_iota(jnp.int32, (bm, 1), 0)
    lo = SUBLANE
    for gi, window in enumerate(POOL_WINDOWS):
        cs = slice(gi * group_dim, (gi + 1) * group_dim)
        src, dst, span = e_ref, p_ref, 1
        while span < window:
            dst[lo:rows, cs] = src[lo:rows, cs] + src[lo - span:rows - span, cs]
            src, dst = dst, (e_ref if dst is p_ref else p_ref)
            span *= 2
        wsum = src[halo:rows, cs]
        cnt = jnp.minimum(pos + 1, window).astype(F32)
        pooled = wsum / cnt - xn[:, cs]
        y = jnp.dot(pooled.astype(BF16), w_ref[gi], preferred_element_type=F32)
        o_ref[:, cs] = x[:, cs] + y * sc_ref[:, cs]
        if window > 2:
            e_ref[halo:rows, cs] = xn[:, cs]

    y = o_ref[...]
    on_ref[...] = (y * lax.rsqrt(jnp.mean(y * y, axis=-1, keepdims=True) + RMS_EPS) * gn_ref[...]).astype(on_ref.dtype)


def pool_mixer(x2d, norm_g, w_group, scale, next_norm_g, seq, bm_pref=256):
    t, d = x2d.shape
    groups, group_dim, _ = w_group.shape
    bm = _tile(seq, bm_pref)
    return pl.pallas_call(
        functools.partial(_pool_kernel, bm=bm, tiles_per_seq=seq // bm, group_dim=group_dim),
        grid=(t // bm,),
        in_specs=[
            pl.BlockSpec((bm, d), lambda i: (i, 0)),
            pl.BlockSpec((1, d), lambda i: (0, 0)),
            pl.BlockSpec((groups, group_dim, group_dim), lambda i: (0, 0, 0)),
            pl.BlockSpec((1, d), lambda i: (0, 0)),
            pl.BlockSpec((1, d), lambda i: (0, 0)),
        ],
        out_specs=[pl.BlockSpec((bm, d), lambda i: (i, 0)), pl.BlockSpec((bm, d), lambda i: (i, 0))],
        out_shape=[jax.ShapeDtypeStruct((t, d), F32), jax.ShapeDtypeStruct((t, d), BF16)],
        scratch_shapes=[pltpu.VMEM((POOL_HALO + bm, d), F32), pltpu.VMEM((POOL_HALO + bm, d), F32)],
        compiler_params=_params(("arbitrary",), 56),
        name="pool_mixer",
    )(x2d, norm_g.reshape(1, d), w_group, scale.reshape(1, d), next_norm_g.reshape(1, d))


def _prep_w_in_kernel(w_ref, o_ref, *, d_ff, fp):
    rows = w_ref.shape[0]
    full = (d_ff // LANE) * LANE
    tail = d_ff - full
    x = w_ref[...]
    o_ref[:, 0:full] = x[:, 0:full].astype(BF16)
    if tail == 0:
        up = x[:, d_ff:2 * d_ff]
    else:
        keep = lax.broadcasted_iota(jnp.int32, (rows, LANE), 1) < tail
        o_ref[:, full:full + LANE] = jnp.where(keep, x[:, full:full + LANE], 0.0).astype(BF16)
        src = x[:, full:2 * d_ff]
        up = pltpu.roll(src, src.shape[1] - tail, axis=1)
        o_ref[:, fp + full:fp + full + LANE] = jnp.where(keep, up[:, full:full + LANE], 0.0).astype(BF16)
    o_ref[:, fp:fp + full] = up[:, 0:full].astype(BF16)
    used = full + (LANE if tail else 0)
    if used < fp:
        zeros = jnp.zeros((rows, fp - used), BF16)
        o_ref[:, used:fp] = zeros
        o_ref[:, fp + used:2 * fp] = zeros


def prep_w_in(w_in_all, layer, fp):
    _, d, two_ff = w_in_all.shape
    d_ff = two_ff // 2
    rows = _tile(d, 128)
    return pl.pallas_call(
        functools.partial(_prep_w_in_kernel, d_ff=d_ff, fp=fp),
        grid=(d // rows,),
        in_specs=[pl.BlockSpec((None, rows, two_ff), lambda i: (layer, i, 0))],
        out_specs=pl.BlockSpec((rows, 2 * fp), lambda i: (i, 0)),
        out_shape=jax.ShapeDtypeStruct((d, 2 * fp), BF16),
        compiler_params=_params(("parallel",), 48),
        name="prep_w_in",
    )(w_in_all)


def _prep_w_out_kernel(w_ref, o_ref, *, n_src):
    @pl.when(pl.program_id(0) < n_src)
    def _():
        o_ref[...] = w_ref[...].astype(BF16)

    @pl.when(pl.program_id(0) >= n_src)
    def _():
        o_ref[...] = jnp.zeros_like(o_ref)


def prep_w_out(w_out_all, layer, fp):
    _, d_ff, d = w_out_all.shape
    unit = 64
    assert d_ff % unit == 0
    units = d_ff // unit
    rows = unit * max(c for c in range(1, units + 1)
                      if units % c == 0 and c * unit * d * 4 <= 10 * MIB)
    n_src = d_ff // rows
    n_blocks = -(-fp // rows)
    return pl.pallas_call(
        functools.partial(_prep_w_out_kernel, n_src=n_src),
        grid=(n_blocks,),
        in_specs=[pl.BlockSpec((None, rows, d), lambda i: (layer, jnp.minimum(i, n_src - 1), 0))],
        out_specs=pl.BlockSpec((rows, d), lambda i: (i, 0)),
        out_shape=jax.ShapeDtypeStruct((n_blocks * rows, d), BF16),
        compiler_params=_params(("parallel",), 40),
        name="prep_w_out",
    )(w_out_all)


class FfnWeightPrep:
    MAX_BLOCK_BYTES = 4 * MIB

    def __init__(self, w_in_all, w_out_all, layer, fp):
        _, d, two_ff = w_in_all.shape
        self.d, self.d_ff, self.fp, self.layer = d, two_ff // 2, fp, layer
        self.arrays = (w_in_all, w_out_all)
        self.out_shapes = [jax.ShapeDtypeStruct((d, 2 * fp), BF16), jax.ShapeDtypeStruct((fp, d), BF16)]

    @staticmethod
    def _rows(total, unit, max_blocks):
        for c in range(1, total // unit + 1):
            if total % (c * unit) == 0 and total // (c * unit) <= max_blocks:
                return c * unit
        return None

    def plan(self, n_steps):
        unit_out = 16
        while self.d_ff % (unit_out * 2) == 0 and self.fp % (unit_out * 2) == 0:
            unit_out *= 2
        self.in_rows = self._rows(self.d, 16, n_steps)
        self.out_rows = self._rows(self.fp, unit_out, n_steps)
        if self.in_rows is None or self.out_rows is None or self.d_ff % self.out_rows:
            return False
        self.n_in, self.n_src, self.n_out = self.d // self.in_rows, self.d_ff // self.out_rows, self.fp // self.out_rows
        return max(self.in_rows * 2 * self.d_ff, self.out_rows * self.d) * 4 <= self.MAX_BLOCK_BYTES

    def specs(self, step):
        layer, n_in, n_src, n_out = self.layer, self.n_in, self.n_src, self.n_out
        ins = [pl.BlockSpec((None, self.in_rows, 2 * self.d_ff),
                            lambda *g: (layer, jnp.minimum(step(*g), n_in - 1), 0)),
               pl.BlockSpec((None, self.out_rows, self.d),
                            lambda *g: (layer, jnp.minimum(step(*g), n_src - 1), 0))]
        outs = [pl.BlockSpec((self.in_rows, 2 * self.fp), lambda *g: (jnp.minimum(step(*g), n_in - 1), 0)),
                pl.BlockSpec((self.out_rows, self.d), lambda *g: (jnp.minimum(step(*g), n_out - 1), 0))]
        return ins, outs

    def body(self, step, in_refs, out_refs):
        _prep_w_in_kernel(in_refs[0], out_refs[0], d_ff=self.d_ff, fp=self.fp)
        real = jnp.minimum(step, self.n_out - 1) < self.n_src
        out_refs[1][...] = jnp.where(real, in_refs[1][...], 0.0).astype(BF16)


def _pad_cols(w, width):
    return jnp.pad(w, [(0, 0)] * (w.ndim - 1) + [(0, width - w.shape[-1])])


FFN_OUT_K_STEPS = 4
MXU_TILE = 256


def _padded_hidden(d_ff):
    width = -(-d_ff // MXU_TILE) * MXU_TILE
    cols = min(FFN_IN_COLS, width)
    stride = -(-width // cols) * cols
    assert stride % (LANE * FFN_OUT_K_STEPS) == 0
    return width, stride


def _conv_ffn(x2d, xn, w_in_all, conv_w, conv_b, w_out_all, layer, seq, prepared=None, side=None):
    d_ff = w_out_all.shape[1]
    width, stride = _padded_hidden(d_ff)
    w_in_p, w_out_p = prepared or (prep_w_in(w_in_all, layer, stride), prep_w_out(w_out_all, layer, stride))
    conv_w_p = _pad_cols(conv_w.reshape(FFN_CONV_WIDTH, 2, d_ff), stride).reshape(FFN_CONV_WIDTH, 2 * stride)
    conv_b_p = _pad_cols(conv_b.reshape(1, 2, d_ff), stride).reshape(1, 2 * stride)
    act = ffn_in(xn, w_in_p, conv_w_p, conv_b_p, width, seq)
    return matmul_residual(act, w_out_p, x2d, FFN_OUT_K_STEPS, k_real=width, side=side)


def _mlstm_layer(x2d, norm_g, w_in, b_i, b_f, head_g, w_out, batch, seq, side=None):
    heads = b_i.shape[0]
    main = w_in.shape[1] - 2 * heads
    xn = rmsnorm(x2d, norm_g, BF16)
    proj, side_out = matmul(xn, w_in.astype(BF16), BF16, n=main, side=side)
    gates, _ = matmul(xn, _pad_cols(w_in[:, main:], LANE).astype(BF16), F32)
    hgated = mlstm_recurrence(proj, gates, b_i, b_f, head_g, batch, seq)
    out, _ = matmul_residual(hgated, w_out.astype(BF16), x2d, 1)
    return out, side_out


def kernel(x, mlstm_norm_g, mlstm_w_in, mlstm_b_i, mlstm_b_f, mlstm_head_g, mlstm_w_out,
           pool_norm_g, pool_w_group, pool_scale,
           ffn_norm_g, ffn_w_in, ffn_conv_w, ffn_conv_b, ffn_w_out, final_norm_g):
    batch, seq, d = x.shape
    depth = ffn_w_in.shape[0]
    _, fp = _padded_hidden(ffn_w_out.shape[1])
    x2d = x.reshape(batch * seq, d)
    prepared = {}
    for layer in range(depth):
        j = layer // 2
        if layer % 2 == 0:
            x2d, prepared[layer] = _mlstm_layer(x2d, mlstm_norm_g[j], mlstm_w_in[j], mlstm_b_i[j], mlstm_b_f[j],
                                                mlstm_head_g[j], mlstm_w_out[j], batch, seq,
                                                side=FfnWeightPrep(ffn_w_in, ffn_w_out, layer, fp))
            xn = rmsnorm(x2d, ffn_norm_g[layer], BF16)
            side = FfnWeightPrep(ffn_w_in, ffn_w_out, layer + 1, fp) if layer + 1 < depth else None
        else:
            x2d, xn = pool_mixer(x2d, pool_norm_g[j], pool_w_group[j].astype(BF16), pool_scale[j],
                                 ffn_norm_g[layer], seq)
            side = None
        x2d, prepared[layer + 1] = _conv_ffn(x2d, xn, ffn_w_in, ffn_conv_w[layer], ffn_conv_b[layer], ffn_w_out,
                                             layer, seq, prepared=prepared.get(layer), side=side)
    return rmsnorm(x2d, final_norm_g, x.dtype).reshape(batch, seq, d)
```

```python
import functools

import jax
import jax.numpy as jnp
from jax import lax
from jax.experimental import pallas as pl
from jax.experimental.pallas import tpu as pltpu

RMS_EPS = 1e-6
GATE_SOFTCAP = 15.0
POOL_WINDOWS = (2, 4, 8, 16)
FFN_CONV_WIDTH = 3
MLSTM_CHUNK = 256
MLSTM_HEADS_PER_STEP = 2
FFN_IN_COLS = 512
LANE = 128
SUBLANE = 8
POOL_HALO = 32
MIB = 1024 * 1024

F32 = jnp.float32
BF16 = jnp.bfloat16


def _params(semantics, vmem_mib):
    return pltpu.CompilerParams(dimension_semantics=semantics, vmem_limit_bytes=vmem_mib * MIB)


def _tile(dim, pref):
    t = min(dim, pref)
    while dim % t:
        t //= 2
    return t


def _rmsnorm_kernel(x_ref, g_ref, o_ref):
    x = x_ref[...]
    y = x * lax.rsqrt(jnp.mean(x * x, axis=-1, keepdims=True) + RMS_EPS)
    o_ref[...] = (y * g_ref[...]).astype(o_ref.dtype)


def rmsnorm(x2d, g, out_dtype):
    t, d = x2d.shape
    bm = _tile(t, 512)
    return pl.pallas_call(
        _rmsnorm_kernel,
        grid=(t // bm,),
        in_specs=[pl.BlockSpec((bm, d), lambda i: (i, 0)), pl.BlockSpec((1, d), lambda i: (0, 0))],
        out_specs=pl.BlockSpec((bm, d), lambda i: (i, 0)),
        out_shape=jax.ShapeDtypeStruct((t, d), out_dtype),
        compiler_params=_params(("parallel",), 48),
        name="rmsnorm",
    )(x2d, g.reshape(1, d))


def _mm_kernel(a_ref, w_ref, *refs, side, n_inner):
    n_side_in = len(side.arrays) if side else 0
    o_ref = refs[n_side_in]
    o_ref[...] = jnp.dot(a_ref[...], w_ref[...], preferred_element_type=F32).astype(o_ref.dtype)
    if side:
        side.body(pl.program_id(0) * n_inner + pl.program_id(1), refs[:n_side_in], refs[n_side_in + 1:])


def matmul(a, w, out_dtype, n=None, side=None, bm_pref=1024, bn_pref=1024):
    m, k = a.shape
    n = w.shape[1] if n is None else n
    bm, bn = _tile(m, bm_pref), _tile(n, bn_pref)
    ni = m // bm
    if side and not side.plan((n // bn) * ni):
        side = None
    step = lambda j, i: j * ni + i
    side_in, side_out = side.specs(step) if side else ([], [])
    outs = pl.pallas_call(
        functools.partial(_mm_kernel, side=side, n_inner=ni),
        grid=(n // bn, ni),
        in_specs=[pl.BlockSpec((bm, k), lambda j, i: (i, 0)), pl.BlockSpec((k, bn), lambda j, i: (0, j))] + side_in,
        out_specs=[pl.BlockSpec((bm, bn), lambda j, i: (i, j))] + side_out,
        out_shape=[jax.ShapeDtypeStruct((m, n), out_dtype)] + (side.out_shapes if side else []),
        compiler_params=_params(("parallel", "arbitrary" if side else "parallel"), 60 if side else 56),
        name="matmul",
    )(a, w, *(side.arrays if side else ()))
    return outs[0], (tuple(outs[1:]) if side else None)


def _mm_res_kernel(a_ref, w_ref, x_ref, *refs, nk, last_bk, side):
    n_side_in = len(side.arrays) if side else 0
    o_ref = refs[n_side_in]

    def partial_product(bk=None):
        if side:
            step = (pl.program_id(0) * pl.num_programs(1) + pl.program_id(1)) * nk + pl.program_id(2)
            side.body(step, refs[:n_side_in], refs[n_side_in + 1:])
        return jnp.dot(a_ref[:, :bk], w_ref[:bk, :], preferred_element_type=F32)

    if nk == 1:
        o_ref[...] = x_ref[...] + partial_product(last_bk)
        return
    kk = pl.program_id(2)

    @pl.when(kk == 0)
    def _():
        o_ref[...] = partial_product()

    @pl.when(jnp.logical_and(kk > 0, kk < nk - 1))
    def _():
        o_ref[...] += partial_product()

    @pl.when(kk == nk - 1)
    def _():
        o_ref[...] = x_ref[...] + (o_ref[...] + partial_product(last_bk))


def matmul_residual(a, w, x, nk, k_real=None, side=None, bm_pref=1024, bn_pref=1024):
    m, k = a.shape
    n = w.shape[1]
    bm, bn, bk = _tile(m, bm_pref), _tile(n, bn_pref), k // nk
    assert bk * nk == k and (nk == 1 or bk % LANE == 0)
    pad = 0 if k_real is None else k - k_real
    last_bk = bk - pad if 0 <= pad < bk and (bk - pad) % LANE == 0 else bk
    nj = n // bn
    if side and not side.plan((m // bm) * nj * nk):
        side = None
    step = lambda i, j, kk: (i * nj + j) * nk + kk
    side_in, side_out = side.specs(step) if side else ([], [])
    outs = pl.pallas_call(
        functools.partial(_mm_res_kernel, nk=nk, last_bk=last_bk, side=side),
        grid=(m // bm, nj, nk),
        in_specs=[
            pl.BlockSpec((bm, bk), lambda i, j, kk: (i, kk)),
            pl.BlockSpec((bk, bn), lambda i, j, kk: (kk, j)),
            pl.BlockSpec((bm, bn), lambda i, j, kk: (i, j)),
        ] + side_in,
        out_specs=[pl.BlockSpec((bm, bn), lambda i, j, kk: (i, j))] + side_out,
        out_shape=[jax.ShapeDtypeStruct((m, n), F32)] + (side.out_shapes if side else []),
        compiler_params=_params(("arbitrary",) * 3 if side else ("parallel", "parallel", "arbitrary"), 56),
        name="matmul_residual",
    )(a, w, x, *(side.arrays if side else ()))
    return outs[0], (tuple(outs[1:]) if side else None)


def _mm_res_prenorm_kernel(a_ref, w_ref, x_ref, g_ref, o_ref, yg_ref, ss_ref):
    y = x_ref[...] + jnp.dot(a_ref[...], w_ref[...], preferred_element_type=F32)
    o_ref[...] = y
    yg_ref[...] = (y * g_ref[...]).astype(yg_ref.dtype)
    part = jnp.broadcast_to(jnp.sum(y * y, axis=1, keepdims=True), ss_ref.shape)

    @pl.when(pl.program_id(1) == 0)
    def _():
        ss_ref[...] = part

    @pl.when(pl.program_id(1) > 0)
    def _():
        ss_ref[...] += part


def matmul_residual_prenorm(a, w, x, g, bm_pref=1024, bn_pref=512):
    m, k = a.shape
    n = w.shape[1]
    bm, bn = _tile(m, bm_pref), _tile(n, bn_pref)
    return pl.pallas_call(
        _mm_res_prenorm_kernel,
        grid=(m // bm, n // bn),
        in_specs=[
            pl.BlockSpec((bm, k), lambda i, j: (i, 0)),
            pl.BlockSpec((k, bn), lambda i, j: (0, j)),
            pl.BlockSpec((bm, bn), lambda i, j: (i, j)),
            pl.BlockSpec((1, bn), lambda i, j: (0, j)),
        ],
        out_specs=[pl.BlockSpec((bm, bn), lambda i, j: (i, j)), pl.BlockSpec((bm, bn), lambda i, j: (i, j)),
                   pl.BlockSpec((bm, LANE), lambda i, j: (i, 0))],
        out_shape=[jax.ShapeDtypeStruct((m, n), F32), jax.ShapeDtypeStruct((m, n), BF16),
                   jax.ShapeDtypeStruct((m, LANE), F32)],
        compiler_params=_params(("parallel", "arbitrary"), 60),
        name="matmul_residual_prenorm",
    )(a, w, x, g.reshape(1, n))


def _as_column(row, length):
    return jnp.transpose(jnp.broadcast_to(row, (LANE, length)))[:, 0:1]


def _cumsum_lanes(row, tri):
    x = jnp.broadcast_to(row, (SUBLANE, row.shape[1]))
    hi = x.astype(BF16).astype(F32)
    r1 = x - hi
    mid = r1.astype(BF16).astype(F32)
    lo = (r1 - mid).astype(BF16).astype(F32)
    parts = jnp.concatenate([hi, mid, lo], axis=0).astype(BF16)
    acc = jnp.dot(parts, tri, preferred_element_type=F32)
    s = acc[0:SUBLANE] + acc[SUBLANE:2 * SUBLANE] + acc[2 * SUBLANE:3 * SUBLANE]
    return s[0:1, :]


def _mlstm_kernel(bi_ref, bf_ref, q_ref, k_ref, v_ref, o_ref, gf_ref, hg_ref, tri_ref, out_ref,
                  c_ref, n_ref, m_ref, *, chunk, dqk, dv, heads_per_step, qk_scale):
    @pl.when(pl.program_id(2) == 0)
    def _():
        c_ref[...] = jnp.zeros_like(c_ref)
        n_ref[...] = jnp.zeros_like(n_ref)
        m_ref[...] = jnp.zeros_like(m_ref)

    tri = tri_ref[...]
    t_idx = lax.broadcasted_iota(jnp.int32, (chunk, chunk), 0)
    s_idx = lax.broadcasted_iota(jnp.int32, (chunk, chunk), 1)
    causal = t_idx >= s_idx

    for hh in range(heads_per_step):
        h = pl.program_id(1) * heads_per_step + hh
        qs = slice(hh * dqk, (hh + 1) * dqk)
        vs = slice(hh * dv, (hh + 1) * dv)
        gf = gf_ref[hh]
        i_row = GATE_SOFTCAP * jnp.tanh((gf[0:1, :] + bi_ref[h]) / GATE_SOFTCAP)
        logf = jax.nn.log_sigmoid(GATE_SOFTCAP * jnp.tanh((gf[1:2, :] + bf_ref[h]) / GATE_SOFTCAP))
        b_row = _cumsum_lanes(logf, tri)
        d_row = i_row - b_row
        b_col = _as_column(b_row, chunk)
        d_col = _as_column(d_row, chunk)

        log_intra = jnp.where(causal, b_col + d_row, -jnp.inf)
        m_prev = m_ref[hh, 0:1, 0:1]
        log_inter = b_col + m_prev
        m_t = jnp.maximum(log_inter, jnp.max(log_intra, axis=1, keepdims=True))
        w_inter = jnp.exp(log_inter - m_t)
        w_intra = jnp.exp(log_intra - m_t)

        q = q_ref[:, qs] * qk_scale
        k = k_ref[:, qs]
        v = v_ref[:, vs]
        s = lax.dot_general(q, k, (((1,), (1,)), ((), ())), preferred_element_type=F32) * w_intra
        c_prev = c_ref[hh]
        n_prev = n_ref[hh]
        num = (w_inter * jnp.dot(q, c_prev.astype(BF16), preferred_element_type=F32)
               + jnp.dot(s.astype(BF16), v, preferred_element_type=F32))
        qn = jnp.sum(q.astype(F32) * n_prev, axis=1, keepdims=True)
        den = w_inter * qn + jnp.sum(s, axis=1, keepdims=True)
        hval = num / jnp.maximum(jnp.abs(den), jnp.exp(-m_t))
        hval = hval * lax.rsqrt(jnp.mean(hval * hval, axis=-1, keepdims=True) + RMS_EPS)
        hval = hval * hg_ref[:, vs]
        out_ref[:, vs] = (hval * jax.nn.sigmoid(o_ref[:, vs].astype(F32))).astype(out_ref.dtype)

        b_last = b_row[:, chunk - 1:chunk]
        log_prev = b_last + m_prev
        m_new = jnp.maximum(log_prev, jnp.max(b_last + d_row, axis=1, keepdims=True))
        a_prev = jnp.exp(log_prev - m_new)
        ka = jnp.exp(b_last + d_col - m_new) * k.astype(F32)
        c_ref[hh] = a_prev * c_prev + lax.dot_general(
            ka.astype(BF16), v, (((0,), (0,)), ((), ())), preferred_element_type=F32)
        n_ref[hh] = a_prev * n_prev + jnp.sum(ka, axis=0, keepdims=True)
        m_ref[hh] = jnp.broadcast_to(m_new, m_ref.shape[1:])


def mlstm_recurrence(proj, gates, b_i, b_f, head_g, batch, seq):
    heads = b_i.shape[0]
    dv = head_g.shape[0] // heads
    dqk = dv // 2
    chunk = _tile(seq, MLSTM_CHUNK)
    nc = seq // chunk
    hps = _tile(heads, MLSTM_HEADS_PER_STEP)
    proj3 = proj.reshape(batch, seq, proj.shape[1])
    gf = gates[:, :2 * heads].reshape(batch, nc, chunk, 2, heads).transpose(0, 4, 1, 3, 2)
    tri = (jnp.arange(chunk)[:, None] <= jnp.arange(chunk)[None, :]).astype(BF16)
    k_off = heads // hps
    v_off = 2 * heads * dqk // (dv * hps)
    o_off = v_off + heads // hps
    smem = pl.BlockSpec(memory_space=pltpu.SMEM)
    out = pl.pallas_call(
        functools.partial(_mlstm_kernel, chunk=chunk, dqk=dqk, dv=dv, heads_per_step=hps,
                          qk_scale=float(dqk) ** -0.5),
        grid=(batch, heads // hps, nc),
        in_specs=[
            smem, smem,
            pl.BlockSpec((None, chunk, hps * dqk), lambda b, h, c: (b, c, h)),
            pl.BlockSpec((None, chunk, hps * dqk), lambda b, h, c: (b, c, k_off + h)),
            pl.BlockSpec((None, chunk, hps * dv), lambda b, h, c: (b, c, v_off + h)),
            pl.BlockSpec((None, chunk, hps * dv), lambda b, h, c: (b, c, o_off + h)),
            pl.BlockSpec((None, hps, None, 2, chunk), lambda b, h, c: (b, h, c, 0, 0)),
            pl.BlockSpec((1, hps * dv), lambda b, h, c: (0, h)),
            pl.BlockSpec((chunk, chunk), lambda b, h, c: (0, 0)),
        ],
        out_specs=pl.BlockSpec((None, chunk, hps * dv), lambda b, h, c: (b, c, h)),
        out_shape=jax.ShapeDtypeStruct((batch, seq, heads * dv), BF16),
        scratch_shapes=[
            pltpu.VMEM((hps, dqk, dv), F32),
            pltpu.VMEM((hps, 1, dqk), F32),
            pltpu.VMEM((hps, SUBLANE, LANE), F32),
        ],
        compiler_params=_params(("parallel", "parallel", "arbitrary"), 48),
        name="mlstm_recurrence",
    )(b_i, b_f, proj3, proj3, proj3, proj3, gf, head_g.reshape(1, heads * dv), tri)
    return out.reshape(batch * seq, heads * dv)


def _ffn_in_kernel(a_ref, wg_ref, wu_ref, cwg_ref, cwu_ref, cbg_ref, cbu_ref, *rest,
                   bm, tiles_per_seq, last_width, scaled):
    ss_ref = rest[0] if scaled else None
    o_ref, hg_ref, hu_ref = rest[-3:]
    first = (pl.program_id(1) % tiles_per_seq) == 0

    @pl.when(first)
    def _():
        hg_ref[0:SUBLANE, :] = jnp.zeros((SUBLANE, hg_ref.shape[1]), F32)
        hu_ref[0:SUBLANE, :] = jnp.zeros((SUBLANE, hu_ref.shape[1]), F32)

    @pl.when(jnp.logical_not(first))
    def _():
        hg_ref[0:SUBLANE, :] = hg_ref[bm:bm + SUBLANE, :]
        hu_ref[0:SUBLANE, :] = hu_ref[bm:bm + SUBLANE, :]

    def tile(width):
        cs = slice(0, width)
        a = a_ref[...]
        if scaled:
            inv_rms = lax.rsqrt(ss_ref[:, 0:1] * (1.0 / a_ref.shape[1]) + RMS_EPS)

        def conv_half(w_ref, cw_ref, cb_ref, h_ref):
            hcur = jnp.dot(a, w_ref[:, cs], preferred_element_type=F32)
            if scaled:
                hcur = hcur * inv_rms
            h_ref[SUBLANE:SUBLANE + bm, cs] = hcur
            hall = h_ref[0:SUBLANE + bm, cs]
            h1 = pltpu.roll(hall, 1, axis=0)[SUBLANE:, :]
            h2 = pltpu.roll(hall, 2, axis=0)[SUBLANE:, :]
            cw = cw_ref[:, cs]
            return cb_ref[:, cs] + cw[0:1, :] * h2 + cw[1:2, :] * h1 + cw[2:3, :] * hcur

        g = conv_half(wg_ref, cwg_ref, cbg_ref, hg_ref)
        u = conv_half(wu_ref, cwu_ref, cbu_ref, hu_ref)
        o_ref[:, cs] = (jax.nn.silu(g) * u).astype(o_ref.dtype)

    bn = o_ref.shape[1]
    if last_width == bn:
        tile(bn)
    else:
        last = pl.program_id(0) == pl.num_programs(0) - 1

        @pl.when(jnp.logical_not(last))
        def _():
            tile(bn)

        @pl.when(last)
        def _():
            tile(last_width)
            o_ref[:, last_width:] = jnp.zeros((bm, bn - last_width), o_ref.dtype)


def ffn_in(a, w_in, conv_w, conv_b, width, seq, row_ss=None, bm_pref=1024):
    t, d = a.shape
    half = w_in.shape[1] // 2
    bm, bn = _tile(seq, bm_pref), min(FFN_IN_COLS, half)
    nj = half // bn
    assert nj * bn == half and (nj - 1) * bn < width <= half and width % LANE == 0
    scaled = row_ss is not None
    return pl.pallas_call(
        functools.partial(_ffn_in_kernel, bm=bm, tiles_per_seq=seq // bm, last_width=width - (nj - 1) * bn,
                          scaled=scaled),
        grid=(nj, t // bm),
        in_specs=[
            pl.BlockSpec((bm, d), lambda j, i: (i, 0)),
            pl.BlockSpec((d, bn), lambda j, i: (0, j)),
            pl.BlockSpec((d, bn), lambda j, i: (0, nj + j)),
            pl.BlockSpec((FFN_CONV_WIDTH, bn), lambda j, i: (0, j)),
            pl.BlockSpec((FFN_CONV_WIDTH, bn), lambda j, i: (0, nj + j)),
            pl.BlockSpec((1, bn), lambda j, i: (0, j)),
            pl.BlockSpec((1, bn), lambda j, i: (0, nj + j)),
        ] + ([pl.BlockSpec((bm, LANE), lambda j, i: (i, 0))] if scaled else []),
        out_specs=pl.BlockSpec((bm, bn), lambda j, i: (i, j)),
        out_shape=jax.ShapeDtypeStruct((t, half), BF16),
        scratch_shapes=[pltpu.VMEM((bm + 2 * SUBLANE, bn), F32), pltpu.VMEM((bm + 2 * SUBLANE, bn), F32)],
        compiler_params=_params(("parallel", "arbitrary"), 56),
        name="ffn_in",
    )(a, w_in, w_in, conv_w, conv_w, conv_b, conv_b, *((row_ss,) if scaled else ()))


def _pool_kernel(x_ref, g_ref, w_ref, sc_ref, gn_ref, o_ref, on_ref, e_ref, p_ref, *, bm, tiles_per_seq, group_dim):
    step = pl.program_id(0)
    tile_in_seq = step % tiles_per_seq
    halo = POOL_HALO
    rows = halo + bm

    @pl.when(step == 0)
    def _():
        e_ref[0:halo, :] = jnp.zeros((halo, e_ref.shape[1]), F32)
        p_ref[0:SUBLANE, :] = jnp.zeros((SUBLANE, p_ref.shape[1]), F32)

    @pl.when(jnp.logical_and(step > 0, tile_in_seq == 0))
    def _():
        e_ref[0:halo, :] = jnp.zeros((halo, e_ref.shape[1]), F32)

    @pl.when(tile_in_seq > 0)
    def _():
        e_ref[0:halo, :] = e_ref[bm:bm + halo, :]

    x = x_ref[...]
    xn = x * lax.rsqrt(jnp.mean(x * x, axis=-1, keepdims=True) + RMS_EPS) * g_ref[...]
    e_ref[halo:rows, :] = xn

    pos = tile_in_seq * bm + lax.broadcasted_iota(jnp.int32, (bm, 1), 0)
    lo = SUBLANE
    for gi, window in enumerate(POOL_WINDOWS):
        cs = slice(gi * group_dim, (gi + 1) * group_dim)
        src, dst, span = e_ref, p_ref, 1
        while span < window:
            dst[lo:rows, cs] = src[lo:rows, cs] + src[lo - span:rows - span, cs]
            src, dst = dst, (e_ref if dst is p_ref else p_ref)
            span *= 2
        wsum = src[halo:rows, cs]
        cnt = jnp.minimum(pos + 1, window).astype(F32)
        pooled = wsum / cnt - xn[:, cs]
        y = jnp.dot(pooled.astype(BF16), w_ref[gi], preferred_element_type=F32)
        o_ref[:, cs] = x[:, cs] + y * sc_ref[:, cs]
        if window > 2:
            e_ref[halo:rows, cs] = xn[:, cs]

    y = o_ref[...]
    on_ref[...] = (y * lax.rsqrt(jnp.mean(y * y, axis=-1, keepdims=True) + RMS_EPS) * gn_ref[...]).astype(on_ref.dtype)


def pool_mixer(x2d, norm_g, w_group, scale, next_norm_g, seq, bm_pref=256):
    t, d = x2d.shape
    groups, group_dim, _ = w_group.shape
    bm = _tile(seq, bm_pref)
    return pl.pallas_call(
        functools.partial(_pool_kernel, bm=bm, tiles_per_seq=seq // bm, group_dim=group_dim),
        grid=(t // bm,),
        in_specs=[
            pl.BlockSpec((bm, d), lambda i: (i, 0)),
            pl.BlockSpec((1, d), lambda i: (0, 0)),
            pl.BlockSpec((groups, group_dim, group_dim), lambda i: (0, 0, 0)),
            pl.BlockSpec((1, d), lambda i: (0, 0)),
            pl.BlockSpec((1, d), lambda i: (0, 0)),
        ],
        out_specs=[pl.BlockSpec((bm, d), lambda i: (i, 0)), pl.BlockSpec((bm, d), lambda i: (i, 0))],
        out_shape=[jax.ShapeDtypeStruct((t, d), F32), jax.ShapeDtypeStruct((t, d), BF16)],
        scratch_shapes=[pltpu.VMEM((POOL_HALO + bm, d), F32), pltpu.VMEM((POOL_HALO + bm, d), F32)],
        compiler_params=_params(("arbitrary",), 56),
        name="pool_mixer",
    )(x2d, norm_g.reshape(1, d), w_group, scale.reshape(1, d), next_norm_g.reshape(1, d))


def _prep_w_in_kernel(w_ref, o_ref, *, d_ff, fp):
    rows = w_ref.shape[0]
    full = (d_ff // LANE) * LANE
    tail = d_ff - full
    x = w_ref[...]
    o_ref[:, 0:full] = x[:, 0:full].astype(BF16)
    if tail == 0:
        up = x[:, d_ff:2 * d_ff]
    else:
        keep = lax.broadcasted_iota(jnp.int32, (rows, LANE), 1) < tail
        o_ref[:, full:full + LANE] = jnp.where(keep, x[:, full:full + LANE], 0.0).astype(BF16)
        src = x[:, full:2 * d_ff]
        up = pltpu.roll(src, src.shape[1] - tail, axis=1)
        o_ref[:, fp + full:fp + full + LANE] = jnp.where(keep, up[:, full:full + LANE], 0.0).astype(BF16)
    o_ref[:, fp:fp + full] = up[:, 0:full].astype(BF16)
    used = full + (LANE if tail else 0)
    if used < fp:
        zeros = jnp.zeros((rows, fp - used), BF16)
        o_ref[:, used:fp] = zeros
        o_ref[:, fp + used:2 * fp] = zeros


def prep_w_in(w_in_all, layer, fp):
    _, d, two_ff = w_in_all.shape
    d_ff = two_ff // 2
    rows = _tile(d, 128)
    return pl.pallas_call(
        functools.partial(_prep_w_in_kernel, d_ff=d_ff, fp=fp),
        grid=(d // rows,),
        in_specs=[pl.BlockSpec((None, rows, two_ff), lambda i: (layer, i, 0))],
        out_specs=pl.BlockSpec((rows, 2 * fp), lambda i: (i, 0)),
        out_shape=jax.ShapeDtypeStruct((d, 2 * fp), BF16),
        compiler_params=_params(("parallel",), 48),
        name="prep_w_in",
    )(w_in_all)


def _prep_w_out_kernel(w_ref, o_ref, *, n_src):
    @pl.when(pl.program_id(0) < n_src)
    def _():
        o_ref[...] = w_ref[...].astype(BF16)

    @pl.when(pl.program_id(0) >= n_src)
    def _():
        o_ref[...] = jnp.zeros_like(o_ref)


def prep_w_out(w_out_all, layer, fp):
    _, d_ff, d = w_out_all.shape
    unit = 64
    assert d_ff % unit == 0
    units = d_ff // unit
    rows = unit * max(c for c in range(1, units + 1)
                      if units % c == 0 and c * unit * d * 4 <= 10 * MIB)
    n_src = d_ff // rows
    n_blocks = -(-fp // rows)
    return pl.pallas_call(
        functools.partial(_prep_w_out_kernel, n_src=n_src),
        grid=(n_blocks,),
        in_specs=[pl.BlockSpec((None, rows, d), lambda i: (layer, jnp.minimum(i, n_src - 1), 0))],
        out_specs=pl.BlockSpec((rows, d), lambda i: (i, 0)),
        out_shape=jax.ShapeDtypeStruct((n_blocks * rows, d), BF16),
        compiler_params=_params(("parallel",), 40),
        name="prep_w_out",
    )(w_out_all)


class FfnWeightPrep:
    MAX_BLOCK_BYTES = 4 * MIB

    def __init__(self, w_in_all, w_out_all, layer, fp):
        _, d, two_ff = w_in_all.shape
        self.d, self.d_ff, self.fp, self.layer = d, two_ff // 2, fp, layer
        self.arrays = (w_in_all, w_out_all)
        self.out_shapes = [jax.ShapeDtypeStruct((d, 2 * fp), BF16), jax.ShapeDtypeStruct((fp, d), BF16)]

    @staticmethod
    def _rows(total, unit, max_blocks):
        for c in range(1, total // unit + 1):
            if total % (c * unit) == 0 and total // (c * unit) <= max_blocks:
                return c * unit
        return None

    def plan(self, n_steps):
        unit_out = 16
        while self.d_ff % (unit_out * 2) == 0 and self.fp % (unit_out * 2) == 0:
            unit_out *= 2
        self.in_rows = self._rows(self.d, 16, n_steps)
        self.out_rows = self._rows(self.fp, unit_out, n_steps)
        if self.in_rows is None or self.out_rows is None or self.d_ff % self.out_rows:
            return False
        self.n_in, self.n_src, self.n_out = self.d // self.in_rows, self.d_ff // self.out_rows, self.fp // self.out_rows
        return max(self.in_rows * 2 * self.d_ff, self.out_rows * self.d) * 4 <= self.MAX_BLOCK_BYTES

    def specs(self, step):
        layer, n_in, n_src, n_out = self.layer, self.n_in, self.n_src, self.n_out
        ins = [pl.BlockSpec((None, self.in_rows, 2 * self.d_ff),
                            lambda *g: (layer, jnp.minimum(step(*g), n_in - 1), 0)),
               pl.BlockSpec((None, self.out_rows, self.d),
                            lambda *g: (layer, jnp.minimum(step(*g), n_src - 1), 0))]
        outs = [pl.BlockSpec((self.in_rows, 2 * self.fp), lambda *g: (jnp.minimum(step(*g), n_in - 1), 0)),
                pl.BlockSpec((self.out_rows, self.d), lambda *g: (jnp.minimum(step(*g), n_out - 1), 0))]
        return ins, outs

    def body(self, step, in_refs, out_refs):
        _prep_w_in_kernel(in_refs[0], out_refs[0], d_ff=self.d_ff, fp=self.fp)
        real = jnp.minimum(step, self.n_out - 1) < self.n_src
        out_refs[1][...] = jnp.where(real, in_refs[1][...], 0.0).astype(BF16)


def _pad_cols(w, width):
    return jnp.pad(w, [(0, 0)] * (w.ndim - 1) + [(0, width - w.shape[-1])])


FFN_OUT_K_STEPS = 4
MXU_TILE = 256


def _padded_hidden(d_ff):
    width = -(-d_ff // MXU_TILE) * MXU_TILE
    cols = min(FFN_IN_COLS, width)
    stride = -(-width // cols) * cols
    assert stride % (LANE * FFN_OUT_K_STEPS) == 0
    return width, stride


def _conv_ffn(x2d, xn, w_in_all, conv_w, conv_b, w_out_all, layer, seq, prepared=None, side=None, row_ss=None):
    d_ff = w_out_all.shape[1]
    width, stride = _padded_hidden(d_ff)
    w_in_p, w_out_p = prepared or (prep_w_in(w_in_all, layer, stride), prep_w_out(w_out_all, layer, stride))
    conv_w_p = _pad_cols(conv_w.reshape(FFN_CONV_WIDTH, 2, d_ff), stride).reshape(FFN_CONV_WIDTH, 2 * stride)
    conv_b_p = _pad_cols(conv_b.reshape(1, 2, d_ff), stride).reshape(1, 2 * stride)
    act = ffn_in(xn, w_in_p, conv_w_p, conv_b_p, width, seq, row_ss=row_ss)
    return matmul_residual(act, w_out_p, x2d, FFN_OUT_K_STEPS, k_real=width, side=side)


def _mlstm_layer(x2d, norm_g, w_in, b_i, b_f, head_g, w_out, next_norm_g, batch, seq, side=None):
    heads = b_i.shape[0]
    main = w_in.shape[1] - 2 * heads
    xn = rmsnorm(x2d, norm_g, BF16)
    proj, side_out = matmul(xn, w_in.astype(BF16), BF16, n=main, side=side)
    gates, _ = matmul(xn, _pad_cols(w_in[:, main:], LANE).astype(BF16), F32)
    hgated = mlstm_recurrence(proj, gates, b_i, b_f, head_g, batch, seq)
    out, out_g, row_ss = matmul_residual_prenorm(hgated, w_out.astype(BF16), x2d, next_norm_g)
    return out, out_g, row_ss, side_out


def kernel(x, mlstm_norm_g, mlstm_w_in, mlstm_b_i, mlstm_b_f, mlstm_head_g, mlstm_w_out,
           pool_norm_g, pool_w_group, pool_scale,
           ffn_norm_g, ffn_w_in, ffn_conv_w, ffn_conv_b, ffn_w_out, final_norm_g):
    batch, seq, d = x.shape
    depth = ffn_w_in.shape[0]
    _, fp = _padded_hidden(ffn_w_out.shape[1])
    x2d = x.reshape(batch * seq, d)
    prepared = {}
    for layer in range(depth):
        j = layer // 2
        if layer % 2 == 0:
            x2d, xn, row_ss, prepared[layer] = _mlstm_layer(
                x2d, mlstm_norm_g[j], mlstm_w_in[j], mlstm_b_i[j], mlstm_b_f[j], mlstm_head_g[j], mlstm_w_out[j],
                ffn_norm_g[layer], batch, seq, side=FfnWeightPrep(ffn_w_in, ffn_w_out, layer, fp))
            side = FfnWeightPrep(ffn_w_in, ffn_w_out, layer + 1, fp) if layer + 1 < depth else None
        else:
            x2d, xn = pool_mixer(x2d, pool_norm_g[j], pool_w_group[j].astype(BF16), pool_scale[j],
                                 ffn_norm_g[layer], seq)
            side, row_ss = None, None
        x2d, prepared[layer + 1] = _conv_ffn(x2d, xn, ffn_w_in, ffn_conv_w[layer], ffn_conv_b[layer], ffn_w_out,
                                             layer, seq, prepared=prepared.get(layer), side=side, row_ss=row_ss)
    return rmsnorm(x2d, final_norm_g, x.dtype).reshape(batch, seq, d)
```

```python
import functools

import jax
import jax.numpy as jnp
from jax import lax
from jax.experimental import pallas as pl
from jax.experimental.pallas import tpu as pltpu

RMS_EPS = 1e-6
GATE_SOFTCAP = 15.0
POOL_WINDOWS = (2, 4, 8, 16)
FFN_CONV_WIDTH = 3
MLSTM_CHUNK = 256
MLSTM_HEADS_PER_STEP = 2
FFN_IN_COLS = 512
LANE = 128
SUBLANE = 8
POOL_HALO = 32
MIB = 1024 * 1024

F32 = jnp.float32
BF16 = jnp.bfloat16


def _params(semantics, vmem_mib):
    return pltpu.CompilerParams(dimension_semantics=semantics, vmem_limit_bytes=vmem_mib * MIB)


def _tile(dim, pref):
    t = min(dim, pref)
    while dim % t:
        t //= 2
    return t


def _rmsnorm_kernel(x_ref, g_ref, o_ref):
    x = x_ref[...]
    y = x * lax.rsqrt(jnp.mean(x * x, axis=-1, keepdims=True) + RMS_EPS)
    o_ref[...] = (y * g_ref[...]).astype(o_ref.dtype)


def rmsnorm(x2d, g, out_dtype):
    t, d = x2d.shape
    bm = _tile(t, 512)
    return pl.pallas_call(
        _rmsnorm_kernel,
        grid=(t // bm,),
        in_specs=[pl.BlockSpec((bm, d), lambda i: (i, 0)), pl.BlockSpec((1, d), lambda i: (0, 0))],
        out_specs=pl.BlockSpec((bm, d), lambda i: (i, 0)),
        out_shape=jax.ShapeDtypeStruct((t, d), out_dtype),
        compiler_params=_params(("parallel",), 48),
        name="rmsnorm",
    )(x2d, g.reshape(1, d))


def _rmsnorm_project_kernel(x_ref, g_ref, w_ref, o_ref, p_ref):
    x = x_ref[...]
    y = x * lax.rsqrt(jnp.mean(x * x, axis=-1, keepdims=True) + RMS_EPS)
    xn = (y * g_ref[...]).astype(o_ref.dtype)
    o_ref[...] = xn
    p_ref[...] = jnp.dot(xn, w_ref[...], preferred_element_type=F32)


def rmsnorm_project(x2d, g, w, out_dtype):
    t, d = x2d.shape
    p = w.shape[1]
    bm = _tile(t, 512)
    return pl.pallas_call(
        _rmsnorm_project_kernel,
        grid=(t // bm,),
        in_specs=[pl.BlockSpec((bm, d), lambda i: (i, 0)), pl.BlockSpec((1, d), lambda i: (0, 0)),
                  pl.BlockSpec((d, p), lambda i: (0, 0))],
        out_specs=[pl.BlockSpec((bm, d), lambda i: (i, 0)), pl.BlockSpec((bm, p), lambda i: (i, 0))],
        out_shape=[jax.ShapeDtypeStruct((t, d), out_dtype), jax.ShapeDtypeStruct((t, p), F32)],
        compiler_params=_params(("parallel",), 48),
        name="rmsnorm_project",
    )(x2d, g.reshape(1, d), w)


def _mm_kernel(a_ref, w_ref, *refs, side, n_inner):
    n_side_in = len(side.arrays) if side else 0
    o_ref = refs[n_side_in]
    o_ref[...] = jnp.dot(a_ref[...], w_ref[...], preferred_element_type=F32).astype(o_ref.dtype)
    if side:
        side.body(pl.program_id(0) * n_inner + pl.program_id(1), refs[:n_side_in], refs[n_side_in + 1:])


def matmul(a, w, out_dtype, n=None, side=None, bm_pref=1024, bn_pref=1024):
    m, k = a.shape
    n = w.shape[1] if n is None else n
    bm, bn = _tile(m, bm_pref), _tile(n, bn_pref)
    ni = m // bm
    if side and not side.plan((n // bn) * ni):
        side = None
    step = lambda j, i: j * ni + i
    side_in, side_out = side.specs(step) if side else ([], [])
    outs = pl.pallas_call(
        functools.partial(_mm_kernel, side=side, n_inner=ni),
        grid=(n // bn, ni),
        in_specs=[pl.BlockSpec((bm, k), lambda j, i: (i, 0)), pl.BlockSpec((k, bn), lambda j, i: (0, j))] + side_in,
        out_specs=[pl.BlockSpec((bm, bn), lambda j, i: (i, j))] + side_out,
        out_shape=[jax.ShapeDtypeStruct((m, n), out_dtype)] + (side.out_shapes if side else []),
        compiler_params=_params(("parallel", "arbitrary" if side else "parallel"), 60 if side else 56),
        name="matmul",
    )(a, w, *(side.arrays if side else ()))
    return outs[0], (tuple(outs[1:]) if side else None)


def _mm_res_kernel(a_ref, w_ref, x_ref, *refs, nk, last_bk, side):
    n_side_in = len(side.arrays) if side else 0
    o_ref = refs[n_side_in]

    def partial_product(bk=None):
        if side:
            step = (pl.program_id(0) * pl.num_programs(1) + pl.program_id(1)) * nk + pl.program_id(2)
            side.body(step, refs[:n_side_in], refs[n_side_in + 1:])
        return jnp.dot(a_ref[:, :bk], w_ref[:bk, :], preferred_element_type=F32)

    if nk == 1:
        o_ref[...] = x_ref[...] + partial_product(last_bk)
        return
    kk = pl.program_id(2)

    @pl.when(kk == 0)
    def _():
        o_ref[...] = partial_product()

    @pl.when(jnp.logical_and(kk > 0, kk < nk - 1))
    def _():
        o_ref[...] += partial_product()

    @pl.when(kk == nk - 1)
    def _():
        o_ref[...] = x_ref[...] + (o_ref[...] + partial_product(last_bk))


def matmul_residual(a, w, x, nk, k_real=None, side=None, bm_pref=1024, bn_pref=1024):
    m, k = a.shape
    n = w.shape[1]
    bm, bn, bk = _tile(m, bm_pref), _tile(n, bn_pref), k // nk
    assert bk * nk == k and (nk == 1 or bk % LANE == 0)
    pad = 0 if k_real is None else k - k_real
    last_bk = bk - pad if 0 <= pad < bk and (bk - pad) % LANE == 0 else bk
    nj = n // bn
    if side and not side.plan((m // bm) * nj * nk):
        side = None
    step = lambda i, j, kk: (i * nj + j) * nk + kk
    side_in, side_out = side.specs(step) if side else ([], [])
    outs = pl.pallas_call(
        functools.partial(_mm_res_kernel, nk=nk, last_bk=last_bk, side=side),
        grid=(m // bm, nj, nk),
        in_specs=[
            pl.BlockSpec((bm, bk), lambda i, j, kk: (i, kk)),
            pl.BlockSpec((bk, bn), lambda i, j, kk: (kk, j)),
            pl.BlockSpec((bm, bn), lambda i, j, kk: (i, j)),
        ] + side_in,
        out_specs=[pl.BlockSpec((bm, bn), lambda i, j, kk: (i, j))] + side_out,
        out_shape=[jax.ShapeDtypeStruct((m, n), F32)] + (side.out_shapes if side else []),
        compiler_params=_params(("arbitrary",) * 3 if side else ("parallel", "parallel", "arbitrary"), 56),
        name="matmul_residual",
    )(a, w, x, *(side.arrays if side else ()))
    return outs[0], (tuple(outs[1:]) if side else None)


def _mm_res_prenorm_kernel(a_ref, w_ref, x_ref, g_ref, o_ref, yg_ref, ss_ref):
    y = x_ref[...] + jnp.dot(a_ref[...], w_ref[...], preferred_element_type=F32)
    o_ref[...] = y
    yg_ref[...] = (y * g_ref[...]).astype(yg_ref.dtype)
    part = jnp.broadcast_to(jnp.sum(y * y, axis=1, keepdims=True), ss_ref.shape)

    @pl.when(pl.program_id(1) == 0)
    def _():
        ss_ref[...] = part

    @pl.when(pl.program_id(1) > 0)
    def _():
        ss_ref[...] += part


def matmul_residual_prenorm(a, w, x, g, bm_pref=1024, bn_pref=512):
    m, k = a.shape
    n = w.shape[1]
    bm, bn = _tile(m, bm_pref), _tile(n, bn_pref)
    return pl.pallas_call(
        _mm_res_prenorm_kernel,
        grid=(m // bm, n // bn),
        in_specs=[
            pl.BlockSpec((bm, k), lambda i, j: (i, 0)),
            pl.BlockSpec((k, bn), lambda i, j: (0, j)),
            pl.BlockSpec((bm, bn), lambda i, j: (i, j)),
            pl.BlockSpec((1, bn), lambda i, j: (0, j)),
        ],
        out_specs=[pl.BlockSpec((bm, bn), lambda i, j: (i, j)), pl.BlockSpec((bm, bn), lambda i, j: (i, j)),
                   pl.BlockSpec((bm, LANE), lambda i, j: (i, 0))],
        out_shape=[jax.ShapeDtypeStruct((m, n), F32), jax.ShapeDtypeStruct((m, n), BF16),
                   jax.ShapeDtypeStruct((m, LANE), F32)],
        compiler_params=_params(("parallel", "arbitrary"), 60),
        name="matmul_residual_prenorm",
    )(a, w, x, g.reshape(1, n))


def _as_column(row, length):
    return jnp.transpose(jnp.broadcast_to(row, (LANE, length)))[:, 0:1]


def _cumsum_lanes(row, tri):
    x = jnp.broadcast_to(row, (SUBLANE, row.shape[1]))
    hi = x.astype(BF16).astype(F32)
    r1 = x - hi
    mid = r1.astype(BF16).astype(F32)
    lo = (r1 - mid).astype(BF16).astype(F32)
    parts = jnp.concatenate([hi, mid, lo], axis=0).astype(BF16)
    acc = jnp.dot(parts, tri, preferred_element_type=F32)
    s = acc[0:SUBLANE] + acc[SUBLANE:2 * SUBLANE] + acc[2 * SUBLANE:3 * SUBLANE]
    return s[0:1, :]


def _mlstm_kernel(bi_ref, bf_ref, q_ref, k_ref, v_ref, o_ref, gf_ref, hg_ref, tri_ref, out_ref,
                  c_ref, n_ref, m_ref, *, chunk, dqk, dv, heads_per_step, qk_scale):
    @pl.when(pl.program_id(2) == 0)
    def _():
        c_ref[...] = jnp.zeros_like(c_ref)
        n_ref[...] = jnp.zeros_like(n_ref)
        m_ref[...] = jnp.zeros_like(m_ref)

    tri = tri_ref[...]
    t_idx = lax.broadcasted_iota(jnp.int32, (chunk, chunk), 0)
    s_idx = lax.broadcasted_iota(jnp.int32, (chunk, chunk), 1)
    causal = t_idx >= s_idx

    for hh in range(heads_per_step):
        h = pl.program_id(1) * heads_per_step + hh
        qs = slice(hh * dqk, (hh + 1) * dqk)
        vs = slice(hh * dv, (hh + 1) * dv)
        gf = gf_ref[hh]
        i_row = GATE_SOFTCAP * jnp.tanh((gf[0:1, :] + bi_ref[h]) / GATE_SOFTCAP)
        logf = jax.nn.log_sigmoid(GATE_SOFTCAP * jnp.tanh((gf[1:2, :] + bf_ref[h]) / GATE_SOFTCAP))
        b_row = _cumsum_lanes(logf, tri)
        d_row = i_row - b_row
        b_col = _as_column(b_row, chunk)
        d_col = _as_column(d_row, chunk)

        log_intra = jnp.where(causal, b_col + d_row, -jnp.inf)
        m_prev = m_ref[hh, 0:1, 0:1]
        log_inter = b_col + m_prev
        m_t = jnp.maximum(log_inter, jnp.max(log_intra, axis=1, keepdims=True))
        w_inter = jnp.exp(log_inter - m_t)
        w_intra = jnp.exp(log_intra - m_t)

        q = q_ref[:, qs] * qk_scale
        k = k_ref[:, qs]
        v = v_ref[:, vs]
        s = lax.dot_general(q, k, (((1,), (1,)), ((), ())), preferred_element_type=F32) * w_intra
        c_prev = c_ref[hh]
        n_prev = n_ref[hh]
        num = (w_inter * jnp.dot(q, c_prev.astype(BF16), preferred_element_type=F32)
               + jnp.dot(s.astype(BF16), v, preferred_element_type=F32))
        qn = jnp.sum(q.astype(F32) * n_prev, axis=1, keepdims=True)
        den = w_inter * qn + jnp.sum(s, axis=1, keepdims=True)
        hval = num / jnp.maximum(jnp.abs(den), jnp.exp(-m_t))
        hval = hval * lax.rsqrt(jnp.mean(hval * hval, axis=-1, keepdims=True) + RMS_EPS)
        hval = hval * hg_ref[:, vs]
        out_ref[:, vs] = (hval * jax.nn.sigmoid(o_ref[:, vs].astype(F32))).astype(out_ref.dtype)

        b_last = b_row[:, chunk - 1:chunk]
        log_prev = b_last + m_prev
        m_new = jnp.maximum(log_prev, jnp.max(b_last + d_row, axis=1, keepdims=True))
        a_prev = jnp.exp(log_prev - m_new)
        ka = jnp.exp(b_last + d_col - m_new) * k.astype(F32)
        c_ref[hh] = a_prev * c_prev + lax.dot_general(
            ka.astype(BF16), v, (((0,), (0,)), ((), ())), preferred_element_type=F32)
        n_ref[hh] = a_prev * n_prev + jnp.sum(ka, axis=0, keepdims=True)
        m_ref[hh] = jnp.broadcast_to(m_new, m_ref.shape[1:])


def mlstm_recurrence(proj, gates, b_i, b_f, head_g, batch, seq):
    heads = b_i.shape[0]
    dv = head_g.shape[0] // heads
    dqk = dv // 2
    chunk = _tile(seq, MLSTM_CHUNK)
    nc = seq // chunk
    hps = _tile(heads, MLSTM_HEADS_PER_STEP)
    proj3 = proj.reshape(batch, seq, proj.shape[1])
    gf = gates[:, :2 * heads].reshape(batch, nc, chunk, 2, heads).transpose(0, 4, 1, 3, 2)
    tri = (jnp.arange(chunk)[:, None] <= jnp.arange(chunk)[None, :]).astype(BF16)
    k_off = heads // hps
    v_off = 2 * heads * dqk // (dv * hps)
    o_off = v_off + heads // hps
    smem = pl.BlockSpec(memory_space=pltpu.SMEM)
    out = pl.pallas_call(
        functools.partial(_mlstm_kernel, chunk=chunk, dqk=dqk, dv=dv, heads_per_step=hps,
                          qk_scale=float(dqk) ** -0.5),
        grid=(batch, heads // hps, nc),
        in_specs=[
            smem, smem,
            pl.BlockSpec((None, chunk, hps * dqk), lambda b, h, c: (b, c, h)),
            pl.BlockSpec((None, chunk, hps * dqk), lambda b, h, c: (b, c, k_off + h)),
            pl.BlockSpec((None, chunk, hps * dv), lambda b, h, c: (b, c, v_off + h)),
            pl.BlockSpec((None, chunk, hps * dv), lambda b, h, c: (b, c, o_off + h)),
            pl.BlockSpec((None, hps, None, 2, chunk), lambda b, h, c: (b, h, c, 0, 0)),
            pl.BlockSpec((1, hps * dv), lambda b, h, c: (0, h)),
            pl.BlockSpec((chunk, chunk), lambda b, h, c: (0, 0)),
        ],
        out_specs=pl.BlockSpec((None, chunk, hps * dv), lambda b, h, c: (b, c, h)),
        out_shape=jax.ShapeDtypeStruct((batch, seq, heads * dv), BF16),
        scratch_shapes=[
            pltpu.VMEM((hps, dqk, dv), F32),
            pltpu.VMEM((hps, 1, dqk), F32),
            pltpu.VMEM((hps, SUBLANE, LANE), F32),
        ],
        compiler_params=_params(("parallel", "parallel", "arbitrary"), 48),
        name="mlstm_recurrence",
    )(b_i, b_f, proj3, proj3, proj3, proj3, gf, head_g.reshape(1, heads * dv), tri)
    return out.reshape(batch * seq, heads * dv)


def _ffn_in_kernel(a_ref, wg_ref, wu_ref, cwg_ref, cwu_ref, cbg_ref, cbu_ref, *rest,
                   bm, tiles_per_seq, last_width, scaled):
    ss_ref = rest[0] if scaled else None
    o_ref, hg_ref, hu_ref = rest[-3:]
    first = (pl.program_id(1) % tiles_per_seq) == 0

    @pl.when(first)
    def _():
        hg_ref[0:SUBLANE, :] = jnp.zeros((SUBLANE, hg_ref.shape[1]), F32)
        hu_ref[0:SUBLANE, :] = jnp.zeros((SUBLANE, hu_ref.shape[1]), F32)

    @pl.when(jnp.logical_not(first))
    def _():
        hg_ref[0:SUBLANE, :] = hg_ref[bm:bm + SUBLANE, :]
        hu_ref[0:SUBLANE, :] = hu_ref[bm:bm + SUBLANE, :]

    def tile(width):
        cs = slice(0, width)
        a = a_ref[...]
        if scaled:
            inv_rms = lax.rsqrt(ss_ref[:, 0:1] * (1.0 / a_ref.shape[1]) + RMS_EPS)

        def conv_half(w_ref, cw_ref, cb_ref, h_ref):
            hcur = jnp.dot(a, w_ref[:, cs], preferred_element_type=F32)
            if scaled:
                hcur = hcur * inv_rms
            h_ref[SUBLANE:SUBLANE + bm, cs] = hcur
            hall = h_ref[0:SUBLANE + bm, cs]
            h1 = pltpu.roll(hall, 1, axis=0)[SUBLANE:, :]
            h2 = pltpu.roll(hall, 2, axis=0)[SUBLANE:, :]
            cw = cw_ref[:, cs]
            return cb_ref[:, cs] + cw[0:1, :] * h2 + cw[1:2, :] * h1 + cw[2:3, :] * hcur

        g = conv_half(wg_ref, cwg_ref, cbg_ref, hg_ref)
        u = conv_half(wu_ref, cwu_ref, cbu_ref, hu_ref)
        o_ref[:, cs] = (jax.nn.silu(g) * u).astype(o_ref.dtype)

    bn = o_ref.shape[1]
    if last_width == bn:
        tile(bn)
    else:
        last = pl.program_id(0) == pl.num_programs(0) - 1

        @pl.when(jnp.logical_not(last))
        def _():
            tile(bn)

        @pl.when(last)
        def _():
            tile(last_width)
            o_ref[:, last_width:] = jnp.zeros((bm, bn - last_width), o_ref.dtype)


def ffn_in(a, w_in, conv_w, conv_b, width, seq, row_ss=None, bm_pref=1024):
    t, d = a.shape
    half = w_in.shape[1] // 2
    bm, bn = _tile(seq, bm_pref), min(FFN_IN_COLS, half)
    nj = half // bn
    assert nj * bn == half and (nj - 1) * bn < width <= half and width % LANE == 0
    scaled = row_ss is not None
    return pl.pallas_call(
        functools.partial(_ffn_in_kernel, bm=bm, tiles_per_seq=seq // bm, last_width=width - (nj - 1) * bn,
                          scaled=scaled),
        grid=(nj, t // bm),
        in_specs=[
            pl.BlockSpec((bm, d), lambda j, i: (i, 0)),
            pl.BlockSpec((d, bn), lambda j, i: (0, j)),
            pl.BlockSpec((d, bn), lambda j, i: (0, nj + j)),
            pl.BlockSpec((FFN_CONV_WIDTH, bn), lambda j, i: (0, j)),
            pl.BlockSpec((FFN_CONV_WIDTH, bn), lambda j, i: (0, nj + j)),
            pl.BlockSpec((1, bn), lambda j, i: (0, j)),
            pl.BlockSpec((1, bn), lambda j, i: (0, nj + j)),
        ] + ([pl.BlockSpec((bm, LANE), lambda j, i: (i, 0))] if scaled else []),
        out_specs=pl.BlockSpec((bm, bn), lambda j, i: (i, j)),
        out_shape=jax.ShapeDtypeStruct((t, half), BF16),
        scratch_shapes=[pltpu.VMEM((bm + 2 * SUBLANE, bn), F32), pltpu.VMEM((bm + 2 * SUBLANE, bn), F32)],
        compiler_params=_params(("parallel", "arbitrary"), 56),
        name="ffn_in",
    )(a, w_in, w_in, conv_w, conv_w, conv_b, conv_b, *((row_ss,) if scaled else ()))


def _pool_kernel(x_ref, g_ref, w_ref, sc_ref, gn_ref, o_ref, on_ref, e_ref, p_ref, *, bm, tiles_per_seq, group_dim):
    step = pl.program_id(0)
    tile_in_seq = step % tiles_per_seq
    halo = POOL_HALO
    rows = halo + bm

    @pl.when(step == 0)
    def _():
        e_ref[0:halo, :] = jnp.zeros((halo, e_ref.shape[1]), F32)
        p_ref[0:SUBLANE, :] = jnp.zeros((SUBLANE, p_ref.shape[1]), F32)

    @pl.when(jnp.logical_and(step > 0, tile_in_seq == 0))
    def _():
        e_ref[0:halo, :] = jnp.zeros((halo, e_ref.shape[1]), F32)

    @pl.when(tile_in_seq > 0)
    def _():
        e_ref[0:halo, :] = e_ref[bm:bm + halo, :]

    x = x_ref[...]
    xn = x * lax.rsqrt(jnp.mean(x * x, axis=-1, keepdims=True) + RMS_EPS) * g_ref[...]
    e_ref[halo:rows, :] = xn

    pos = tile_in_seq * bm + lax.broadcasted_iota(jnp.int32, (bm, 1), 0)
    lo = SUBLANE
    for gi, window in enumerate(POOL_WINDOWS):
        cs = slice(gi * group_dim, (gi + 1) * group_dim)
        src, dst, span = e_ref, p_ref, 1
        while span < window:
            dst[lo:rows, cs] = src[lo:rows, cs] + src[lo - span:rows - span, cs]
            src, dst = dst, (e_ref if dst is p_ref else p_ref)
            span *= 2
        wsum = src[halo:rows, cs]
        cnt = jnp.minimum(pos + 1, window).astype(F32)
        pooled = wsum / cnt - xn[:, cs]
        y = jnp.dot(pooled.astype(BF16), w_ref[gi], preferred_element_type=F32)
        o_ref[:, cs] = x[:, cs] + y * sc_ref[:, cs]
        if window > 2:
            e_ref[halo:rows, cs] = xn[:, cs]

    y = o_ref[...]
    on_ref[...] = (y * lax.rsqrt(jnp.mean(y * y, axis=-1, keepdims=True) + RMS_EPS) * gn_ref[...]).astype(on_ref.dtype)


def pool_mixer(x2d, norm_g, w_group, scale, next_norm_g, seq, bm_pref=256):
    t, d = x2d.shape
    groups, group_dim, _ = w_group.shape
    bm = _tile(seq, bm_pref)
    return pl.pallas_call(
        functools.partial(_pool_kernel, bm=bm, tiles_per_seq=seq // bm, group_dim=group_dim),
        grid=(t // bm,),
        in_specs=[
            pl.BlockSpec((bm, d), lambda i: (i, 0)),
            pl.BlockSpec((1, d), lambda i: (0, 0)),
            pl.BlockSpec((groups, group_dim, group_dim), lambda i: (0, 0, 0)),
            pl.BlockSpec((1, d), lambda i: (0, 0)),
            pl.BlockSpec((1, d), lambda i: (0, 0)),
        ],
        out_specs=[pl.BlockSpec((bm, d), lambda i: (i, 0)), pl.BlockSpec((bm, d), lambda i: (i, 0))],
        out_shape=[jax.ShapeDtypeStruct((t, d), F32), jax.ShapeDtypeStruct((t, d), BF16)],
        scratch_shapes=[pltpu.VMEM((POOL_HALO + bm, d), F32), pltpu.VMEM((POOL_HALO + bm, d), F32)],
        compiler_params=_params(("arbitrary",), 56),
        name="pool_mixer",
    )(x2d, norm_g.reshape(1, d), w_group, scale.reshape(1, d), next_norm_g.reshape(1, d))


def _prep_w_in_kernel(w_ref, o_ref, *, d_ff, fp):
    rows = w_ref.shape[0]
    full = (d_ff // LANE) * LANE
    tail = d_ff - full
    x = w_ref[...]
    o_ref[:, 0:full] = x[:, 0:full].astype(BF16)
    if tail == 0:
        up = x[:, d_ff:2 * d_ff]
    else:
        keep = lax.broadcasted_iota(jnp.int32, (rows, LANE), 1) < tail
        o_ref[:, full:full + LANE] = jnp.where(keep, x[:, full:full + LANE], 0.0).astype(BF16)
        src = x[:, full:2 * d_ff]
        up = pltpu.roll(src, src.shape[1] - tail, axis=1)
        o_ref[:, fp + full:fp + full + LANE] = jnp.where(keep, up[:, full:full + LANE], 0.0).astype(BF16)
    o_ref[:, fp:fp + full] = up[:, 0:full].astype(BF16)
    used = full + (LANE if tail else 0)
    if used < fp:
        zeros = jnp.zeros((rows, fp - used), BF16)
        o_ref[:, used:fp] = zeros
        o_ref[:, fp + used:2 * fp] = zeros


def prep_w_in(w_in_all, layer, fp):
    _, d, two_ff = w_in_all.shape
    d_ff = two_ff // 2
    rows = _tile(d, 128)
    return pl.pallas_call(
        functools.partial(_prep_w_in_kernel, d_ff=d_ff, fp=fp),
        grid=(d // rows,),
        in_specs=[pl.BlockSpec((None, rows, two_ff), lambda i: (layer, i, 0))],
        out_specs=pl.BlockSpec((rows, 2 * fp), lambda i: (i, 0)),
        out_shape=jax.ShapeDtypeStruct((d, 2 * fp), BF16),
        compiler_params=_params(("parallel",), 48),
        name="prep_w_in",
    )(w_in_all)


def _prep_w_out_kernel(w_ref, o_ref, *, n_src):
    @pl.when(pl.program_id(0) < n_src)
    def _():
        o_ref[...] = w_ref[...].astype(BF16)

    @pl.when(pl.program_id(0) >= n_src)
    def _():
        o_ref[...] = jnp.zeros_like(o_ref)


def prep_w_out(w_out_all, layer, fp):
    _, d_ff, d = w_out_all.shape
    unit = 64
    assert d_ff % unit == 0
    units = d_ff // unit
    rows = unit * max(c for c in range(1, units + 1)
                      if units % c == 0 and c * unit * d * 4 <= 10 * MIB)
    n_src = d_ff // rows
    n_blocks = -(-fp // rows)
    return pl.pallas_call(
        functools.partial(_prep_w_out_kernel, n_src=n_src),
        grid=(n_blocks,),
        in_specs=[pl.BlockSpec((None, rows, d), lambda i: (layer, jnp.minimum(i, n_src - 1), 0))],
        out_specs=pl.BlockSpec((rows, d), lambda i: (i, 0)),
        out_shape=jax.ShapeDtypeStruct((n_blocks * rows, d), BF16),
        compiler_params=_params(("parallel",), 40),
        name="prep_w_out",
    )(w_out_all)


class FfnWeightPrep:
    MAX_BLOCK_BYTES = 4 * MIB

    def __init__(self, w_in_all, w_out_all, layer, fp):
        _, d, two_ff = w_in_all.shape
        self.d, self.d_ff, self.fp, self.layer = d, two_ff // 2, fp, layer
        self.arrays = (w_in_all, w_out_all)
        self.out_shapes = [jax.ShapeDtypeStruct((d, 2 * fp), BF16), jax.ShapeDtypeStruct((fp, d), BF16)]

    @staticmethod
    def _rows(total, unit, max_blocks):
        for c in range(1, total // unit + 1):
            if total % (c * unit) == 0 and total // (c * unit) <= max_blocks:
                return c * unit
        return None

    def plan(self, n_steps):
        unit_out = 16
        while self.d_ff % (unit_out * 2) == 0 and self.fp % (unit_out * 2) == 0:
            unit_out *= 2
        self.in_rows = self._rows(self.d, 16, n_steps)
        self.out_rows = self._rows(self.fp, unit_out, n_steps)
        if self.in_rows is None or self.out_rows is None or self.d_ff % self.out_rows:
            return False
        self.n_in, self.n_src, self.n_out = self.d // self.in_rows, self.d_ff // self.out_rows, self.fp // self.out_rows
        return max(self.in_rows * 2 * self.d_ff, self.out_rows * self.d) * 4 <= self.MAX_BLOCK_BYTES

    def specs(self, step):
        layer, n_in, n_src, n_out = self.layer, self.n_in, self.n_src, self.n_out
        ins = [pl.BlockSpec((None, self.in_rows, 2 * self.d_ff),
                            lambda *g: (layer, jnp.minimum(step(*g), n_in - 1), 0)),
               pl.BlockSpec((None, self.out_rows, self.d),
                            lambda *g: (layer, jnp.minimum(step(*g), n_src - 1), 0))]
        outs = [pl.BlockSpec((self.in_rows, 2 * self.fp), lambda *g: (jnp.minimum(step(*g), n_in - 1), 0)),
                pl.BlockSpec((self.out_rows, self.d), lambda *g: (jnp.minimum(step(*g), n_out - 1), 0))]
        return ins, outs

    def body(self, step, in_refs, out_refs):
        _prep_w_in_kernel(in_refs[0], out_refs[0], d_ff=self.d_ff, fp=self.fp)
        real = jnp.minimum(step, self.n_out - 1) < self.n_src
        out_refs[1][...] = jnp.where(real, in_refs[1][...], 0.0).astype(BF16)


def _pad_cols(w, width):
    return jnp.pad(w, [(0, 0)] * (w.ndim - 1) + [(0, width - w.shape[-1])])


FFN_OUT_K_STEPS = 4
MXU_TILE = 256


def _padded_hidden(d_ff):
    width = -(-d_ff // MXU_TILE) * MXU_TILE
    cols = min(FFN_IN_COLS, width)
    stride = -(-width // cols) * cols
    assert stride % (LANE * FFN_OUT_K_STEPS) == 0
    return width, stride


def _conv_ffn(x2d, xn, w_in_all, conv_w, conv_b, w_out_all, layer, seq, prepared=None, side=None, row_ss=None):
    d_ff = w_out_all.shape[1]
    width, stride = _padded_hidden(d_ff)
    w_in_p, w_out_p = prepared or (prep_w_in(w_in_all, layer, stride), prep_w_out(w_out_all, layer, stride))
    conv_w_p = _pad_cols(conv_w.reshape(FFN_CONV_WIDTH, 2, d_ff), stride).reshape(FFN_CONV_WIDTH, 2 * stride)
    conv_b_p = _pad_cols(conv_b.reshape(1, 2, d_ff), stride).reshape(1, 2 * stride)
    act = ffn_in(xn, w_in_p, conv_w_p, conv_b_p, width, seq, row_ss=row_ss)
    return matmul_residual(act, w_out_p, x2d, FFN_OUT_K_STEPS, k_real=width, side=side)


def _mlstm_layer(x2d, norm_g, w_in, b_i, b_f, head_g, w_out, next_norm_g, batch, seq, side=None):
    heads = b_i.shape[0]
    main = w_in.shape[1] - 2 * heads
    xn, gates = rmsnorm_project(x2d, norm_g, _pad_cols(w_in[:, main:], LANE).astype(BF16), BF16)
    proj, side_out = matmul(xn, w_in.astype(BF16), BF16, n=main, side=side)
    hgated = mlstm_recurrence(proj, gates, b_i, b_f, head_g, batch, seq)
    out, out_g, row_ss = matmul_residual_prenorm(hgated, w_out.astype(BF16), x2d, next_norm_g)
    return out, out_g, row_ss, side_out


def kernel(x, mlstm_norm_g, mlstm_w_in, mlstm_b_i, mlstm_b_f, mlstm_head_g, mlstm_w_out,
           pool_norm_g, pool_w_group, pool_scale,
           ffn_norm_g, ffn_w_in, ffn_conv_w, ffn_conv_b, ffn_w_out, final_norm_g):
    batch, seq, d = x.shape
    depth = ffn_w_in.shape[0]
    _, fp = _padded_hidden(ffn_w_out.shape[1])
    x2d = x.reshape(batch * seq, d)
    prepared = {}
    for layer in range(depth):
        j = layer // 2
        if layer % 2 == 0:
            x2d, xn, row_ss, prepared[layer] = _mlstm_layer(
                x2d, mlstm_norm_g[j], mlstm_w_in[j], mlstm_b_i[j], mlstm_b_f[j], mlstm_head_g[j], mlstm_w_out[j],
                ffn_norm_g[layer], batch, seq, side=FfnWeightPrep(ffn_w_in, ffn_w_out, layer, fp))
            side = FfnWeightPrep(ffn_w_in, ffn_w_out, layer + 1, fp) if layer + 1 < depth else None
        else:
            x2d, xn = pool_mixer(x2d, pool_norm_g[j], pool_w_group[j].astype(BF16), pool_scale[j],
                                 ffn_norm_g[layer], seq)
            side, row_ss = None, None
        x2d, prepared[layer + 1] = _conv_ffn(x2d, xn, ffn_w_in, ffn_conv_w[layer], ffn_conv_b[layer], ffn_w_out,
                                             layer, seq, prepared=prepared.get(layer), side=side, row_ss=row_ss)
    return rmsnorm(x2d, final_norm_g, x.dtype).reshape(batch, seq, d)
```
